```python
import math
import jax, jax.numpy as jnp
from jax import lax
import numpy as np

D_MODEL = 2048
BATCH = 2
SEQ = 4096
DEPTH = 4
DEC_BATCH = 8
DEC_SEQ = 8
PAST_LEN = 16384
PAGE_SIZE = 128

H_A = 8
DK_A = 128
DV_A = 128
F_A = H_A * DK_A
W_A = H_A * DV_A
H_B = 8
DH_B = 128
W_B = H_B * DH_B
CHUNK = 64
Q_BLOCK = 128
N_IN = 2 * F_A + 2 * W_A + 4 * W_B + H_B + 2 * D_MODEL
RMS_EPS = 1e-6

kernel_name = 'hgrn2_fox_parallel_gated_decoder_step'


def _rms_norm(x, g):
    xf = x.astype(jnp.float32)
    y = xf * lax.rsqrt(jnp.mean(xf * xf, axis=-1, keepdims=True) + RMS_EPS)
    return (y * g.astype(jnp.float32)).astype(x.dtype)


def _in_proj(h, w_in_l, f_bias_l):
    B, L, _ = h.shape
    p = jnp.einsum('bld,dn->bln', h, w_in_l)
    sizes = [F_A, F_A, W_A, W_A, W_B, W_B, W_B, W_B, H_B, D_MODEL, D_MODEL]
    cuts = []
    acc = 0
    for s in sizes[:-1]:
        acc += s
        cuts.append(acc)
    qa, fa, ia, za, qb, kb, vb, zb, fb, ga, gb = jnp.split(p, cuts, axis=-1)
    logf_b = jax.nn.log_sigmoid((fb + f_bias_l).astype(jnp.float32))
    return (qa.reshape(B, L, H_A, DK_A), fa.reshape(B, L, H_A, DK_A), ia.reshape(B, L, H_A, DV_A), za,
            qb.reshape(B, L, H_B, DH_B), kb.reshape(B, L, H_B, DH_B), vb.reshape(B, L, H_B, DH_B), zb,
            logf_b, ga, gb)


def _hgrn2(q, f_logit, i, lb, s0):
    B, L = q.shape[0], q.shape[1]
    c = math.gcd(L, CHUNK)
    n = L // c
    xf = f_logit.astype(jnp.float32)
    lbf = lb.astype(jnp.float32)
    logf = jnp.log(lbf + (1.0 - lbf) * jax.nn.sigmoid(xf))
    k = (1.0 - lbf) * jax.nn.sigmoid(-xf)
    qf = jax.nn.silu(q.astype(jnp.float32))

    def chunks(t):
        return t.reshape(B, n, c, H_A, t.shape[-1]).transpose(1, 0, 3, 2, 4)

    mask = jnp.tril(jnp.ones((c, c), dtype=bool))[None, None, :, :, None]

    def step(S, inp):
        qc, kc, vc, gc = inp
        G = jnp.cumsum(gc, axis=2)
        o_inter = jnp.einsum('bhtk,bhkv->bhtv', qc * jnp.exp(G), S)
        decay = jnp.exp(jnp.where(mask, G[:, :, :, None, :] - G[:, :, None, :, :], -jnp.inf))
        A = jnp.einsum('bhtk,bhtsk,bhsk->bhts', qc, decay, kc)
        o = o_inter + jnp.einsum('bhts,bhsv->bhtv', A, vc)
        G_end = G[:, :, -1:, :]
        S_new = jnp.exp(G_end[:, :, 0, :])[..., None] * S + jnp.einsum(
            'bhsk,bhsv->bhkv', kc * jnp.exp(G_end - G), vc)
        return S_new, o

    S_fin, o = lax.scan(step, s0.astype(jnp.float32),
                        (chunks(qf), chunks(k), chunks(i.astype(jnp.float32)), chunks(logf)))
    o = o.transpose(1, 0, 3, 2, 4).reshape(B, L, H_A, DV_A)
    return o, S_fin


def _fox_prompt(q, k, v, logf):
    B, S = q.shape[0], q.shape[1]
    nb = S // Q_BLOCK
    scale = DH_B ** -0.5
    c = jnp.cumsum(logf, axis=1).transpose(0, 2, 1)
    qb = q.reshape(B, nb, Q_BLOCK, H_B, DH_B).transpose(1, 0, 2, 3, 4)
    cb = c.reshape(B, H_B, nb, Q_BLOCK).transpose(2, 0, 1, 3)
    kpos = jnp.arange(S)

    def one_block(args):
        qi, ci, bi = args
        s = jnp.einsum('bqhd,bkhd->bhqk', qi, k).astype(jnp.float32) * scale
        s = s + ci[..., None] - c[:, :, None, :]
        qpos = bi * Q_BLOCK + jnp.arange(Q_BLOCK)
        s = jnp.where((kpos[None, :] <= qpos[:, None])[None, None], s, -jnp.inf)
        p = jax.nn.softmax(s, axis=-1)
        return jnp.einsum('bhqk,bkhd->bqhd', p.astype(v.dtype), v)

    o = lax.map(one_block, (qb, cb, jnp.arange(nb)))
    return o.transpose(1, 0, 2, 3, 4).reshape(B, S, H_B, DH_B)


def _fox_sample(q, k_new, v_new, logf_new, k_past, v_past, logf_past):
    T = q.shape[1]
    scale = DH_B ** -0.5
    lfp = logf_past.astype(jnp.float32)
    r = (lax.cumsum(lfp, axis=1, reverse=True) - lfp).transpose(0, 2, 1)
    cn = jnp.cumsum(logf_new.astype(jnp.float32), axis=1).transpose(0, 2, 1)
    s_past = jnp.einsum('bthd,bshd->bhts', q, k_past).astype(jnp.float32) * scale
    s_past = s_past + cn[..., None] + r[:, :, None, :]
    s_new = jnp.einsum('bthd,bshd->bhts', q, k_new).astype(jnp.float32) * scale
    s_new = s_new + cn[..., None] - cn[:, :, None, :]
    s_new = jnp.where(jnp.tril(jnp.ones((T, T), dtype=bool))[None, None], s_new, -jnp.inf)
    p = jax.nn.softmax(jnp.concatenate([s_past, s_new], axis=-1), axis=-1).astype(v_new.dtype)
    P = k_past.shape[1]
    return (jnp.einsum('bhts,bshd->bthd', p[..., :P], v_past)
            + jnp.einsum('bhts,bshd->bthd', p[..., P:], v_new))


def _merge(oa, za, ob, zb, ga, gb, norm_g_l, w_a_l, w_b_l, w_out_l):
    B, L = za.shape[0], za.shape[1]
    ya = _rms_norm(oa, norm_g_l).reshape(B, L, W_A).astype(za.dtype) * jax.nn.silu(za)
    yb = ob.reshape(B, L, W_B) * jax.nn.silu(zb)
    u = (jax.nn.sigmoid(ga) * jnp.einsum('blc,cd->bld', ya, w_a_l)
         + jax.nn.sigmoid(gb) * jnp.einsum('blc,cd->bld', yb, w_b_l))
    return jnp.einsum('bld,de->ble', u, w_out_l)


def setup_inputs(seed: int = 0) -> dict:
    key = jax.random.key(seed)
    ks = jax.random.split(key, 16)
    n_pages = PAST_LEN // PAGE_SIZE
    n_used = DEC_BATCH * n_pages
    n_pool = n_used + (n_used + 3) // 4
    nrm = jax.random.normal
    x_prompt = nrm(ks[0], (BATCH, SEQ, D_MODEL), jnp.float32)
    x_sample = nrm(ks[1], (DEC_BATCH, DEC_SEQ, D_MODEL), jnp.float32)
    state_hgrn = 0.5 * nrm(ks[2], (DEPTH, DEC_BATCH, H_A, DK_A, DV_A), jnp.float32)
    cache_k = nrm(ks[3], (DEPTH, n_pool, PAGE_SIZE, H_B, DH_B), jnp.float32)
    cache_v = nrm(ks[4], (DEPTH, n_pool, PAGE_SIZE, H_B, DH_B), jnp.float32)
    cache_logf = jax.nn.log_sigmoid(3.0 + nrm(ks[5], (DEPTH, n_pool, PAGE_SIZE, H_B), jnp.float32))
    page_table = jax.random.permutation(ks[6], n_pool)[:n_used].reshape(DEC_BATCH, n_pages).astype(jnp.int32)
    w_in = nrm(ks[7], (DEPTH, D_MODEL, N_IN), jnp.float32) * D_MODEL ** -0.5
    hgrn_lower_bounds = 1.0 + 0.1 * nrm(ks[8], (DEPTH, F_A), jnp.float32)
    hgrn_norm_g = 1.0 + 0.05 * nrm(ks[9], (DEPTH, DV_A), jnp.float32)
    fox_f_bias = jax.random.uniform(ks[10], (DEPTH, H_B), jnp.float32, minval=2.0, maxval=6.0)
    w_branch_a = nrm(ks[11], (DEPTH, W_A, D_MODEL), jnp.float32) * W_A ** -0.5
    w_branch_b = nrm(ks[12], (DEPTH, W_B, D_MODEL), jnp.float32) * W_B ** -0.5
    w_out = nrm(ks[13], (DEPTH, D_MODEL, D_MODEL), jnp.float32) * D_MODEL ** -0.5
    pre_norm_g = 1.0 + 0.05 * nrm(ks[14], (DEPTH, D_MODEL), jnp.float32)
    post_norm_g = 1.0 + 0.05 * nrm(ks[15], (DEPTH, D_MODEL), jnp.float32)
    return {'x_prompt': x_prompt, 'x_sample': x_sample, 'state_hgrn': state_hgrn,
            'cache_k': cache_k, 'cache_v': cache_v, 'cache_logf': cache_logf, 'page_table': page_table,
            'w_in': w_in, 'hgrn_lower_bounds': hgrn_lower_bounds, 'hgrn_norm_g': hgrn_norm_g,
            'fox_f_bias': fox_f_bias, 'w_branch_a': w_branch_a, 'w_branch_b': w_branch_b,
            'w_out': w_out, 'pre_norm_g': pre_norm_g, 'post_norm_g': post_norm_g}


def reference(x_prompt, x_sample, state_hgrn, cache_k, cache_v, cache_logf, page_table,
              w_in, hgrn_lower_bounds, hgrn_norm_g, fox_f_bias, w_branch_a, w_branch_b,
              w_out, pre_norm_g, post_norm_g):
    lb_all = jnp.cumsum(jax.nn.softmax(hgrn_lower_bounds.astype(jnp.float32), axis=0), axis=0)
    lb_all = (lb_all - lb_all[0:1]).reshape(DEPTH, H_A, DK_A)
    bp = x_prompt.shape[0]
    bd, n_pages = page_table.shape
    past = n_pages * PAGE_SIZE
    xp, xs = x_prompt, x_sample
    st_p, st_s, kp_l, vp_l, lfp_l, ks_l, vs_l, lfs_l = [], [], [], [], [], [], [], []
    for l in range(DEPTH):
        h = _rms_norm(xp, pre_norm_g[l])
        qa, fa, ia, za, qb, kb, vb, zb, lfb, ga, gb = _in_proj(h, w_in[l], fox_f_bias[l])
        s0 = jnp.zeros((bp, H_A, DK_A, DV_A), jnp.float32)
        oa, s_fin = _hgrn2(qa, fa, ia, lb_all[l], s0)
        ob = _fox_prompt(qb, kb, vb, lfb)
        mix = _merge(oa, za, ob, zb, ga, gb, hgrn_norm_g[l], w_branch_a[l], w_branch_b[l], w_out[l])
        xp = xp + _rms_norm(mix, post_norm_g[l])
        st_p.append(s_fin.astype(x_prompt.dtype))
        kp_l.append(kb)
        vp_l.append(vb)
        lfp_l.append(lfb.astype(x_prompt.dtype))
        h = _rms_norm(xs, pre_norm_g[l])
        qa, fa, ia, za, qb, kb, vb, zb, lfb, ga, gb = _in_proj(h, w_in[l], fox_f_bias[l])
        oa, s_fin = _hgrn2(qa, fa, ia, lb_all[l], state_hgrn[l])
        k_past = cache_k[l][page_table].reshape(bd, past, H_B, DH_B)
        v_past = cache_v[l][page_table].reshape(bd, past, H_B, DH_B)
        lf_past = cache_logf[l][page_table].reshape(bd, past, H_B)
        ob = _fox_sample(qb, kb, vb, lfb, k_past, v_past, lf_past)
        mix = _merge(oa, za, ob, zb, ga, gb, hgrn_norm_g[l], w_branch_a[l], w_branch_b[l], w_out[l])
        xs = xs + _rms_norm(mix, post_norm_g[l])
        st_s.append(s_fin.astype(state_hgrn.dtype))
        ks_l.append(kb)
        vs_l.append(vb)
        lfs_l.append(lfb.astype(cache_logf.dtype))
    hgrn_state_prompt = jnp.stack(st_p)
    hgrn_state_sample = jnp.stack(st_s)
    k_prompt = jnp.stack(kp_l)
    v_prompt = jnp.stack(vp_l)
    logf_prompt = jnp.stack(lfp_l)
    k_sample = jnp.stack(ks_l)
    v_sample = jnp.stack(vs_l)
    logf_sample = jnp.stack(lfs_l)
    return (xp, xs, hgrn_state_prompt, hgrn_state_sample,
            k_prompt, v_prompt, logf_prompt, k_sample, v_sample, logf_sample)
```

```python
import functools

import jax
import jax.numpy as jnp
from jax import lax
from jax.experimental import pallas as pl
from jax.experimental.pallas import tpu as pltpu

F32 = jnp.float32
BF16 = jnp.bfloat16

RMS_EPS = 1e-6
HGRN_CHUNK = 64
LANES = 128
VMEM_LIMIT_BYTES = 60000 * 1024


def _params(*sem):
    return pltpu.CompilerParams(dimension_semantics=sem, vmem_limit_bytes=VMEM_LIMIT_BYTES)


def _tile(n, pref):
    if n <= pref:
        return n
    t = pref
    while n % t:
        t //= 2
    return t


def _dot(a, b):
    return jnp.dot(a, b, preferred_element_type=F32)


def _dot_nt(a, b):
    return lax.dot_general(a, b, (((1,), (1,)), ((), ())), preferred_element_type=F32)


def _dot_tn(a, b):
    return lax.dot_general(a, b, (((0,), (0,)), ((), ())), preferred_element_type=F32)


def _split3(a):
    hi = a.astype(BF16)
    r = a - hi.astype(F32)
    mid = r.astype(BF16)
    lo = (r - mid.astype(F32)).astype(BF16)
    return hi, mid, lo


def _dot01(m01, a):
    hi, mid, lo = _split3(a)
    return _dot(m01, hi) + _dot(m01, mid) + _dot(m01, lo)


def _dot01_right(a, m01):
    hi, mid, lo = _split3(a)
    return _dot(hi, m01) + _dot(mid, m01) + _dot(lo, m01)


def _sigmoid(x):
    return 1.0 / (1.0 + jnp.exp(-x))


def _log_sigmoid(z):
    return jnp.minimum(z, 0.0) - jnp.log(1.0 + jnp.exp(-jnp.abs(z)))


def _lb_kernel(x_ref, o_ref):
    x = x_ref[...]
    depth = x.shape[0]
    e = jnp.exp(x - jnp.max(x, axis=0, keepdims=True))
    sm = e / jnp.sum(e, axis=0, keepdims=True)
    acc = jnp.zeros_like(sm[0:1])
    for i in range(depth):
        o_ref[i:i + 1, :] = acc
        if i + 1 < depth:
            acc = acc + sm[i + 1:i + 2]


def _lower_bounds(lb_param):
    return pl.pallas_call(
        _lb_kernel,
        out_shape=jax.ShapeDtypeStruct(lb_param.shape, F32),
        name="hgrn_lower_bounds",
    )(lb_param.astype(F32))


def _inproj_kernel(x_ref, g_ref, w_ref, wf_ref, fb_ref, p_ref, lf_ref, h_scr):
    @pl.when(pl.program_id(1) == 0)
    def _():
        x = x_ref[...]
        h = x * lax.rsqrt(jnp.mean(x * x, axis=-1, keepdims=True) + RMS_EPS) * g_ref[...]
        hb = h.astype(BF16)
        h_scr[...] = hb
        lf_ref[...] = _log_sigmoid(_dot(hb, wf_ref[...]) + fb_ref[...])

    p_ref[...] = _dot(h_scr[...], w_ref[...])


def _in_proj(x, pre_g, w_cat, w_f, f_bias, *, tm_pref):
    m, d = x.shape
    n = w_cat.shape[1]
    tm = _tile(m, tm_pref)
    tn = _tile(n, 1024)
    return pl.pallas_call(
        _inproj_kernel,
        grid=(m // tm, n // tn),
        in_specs=[
            pl.BlockSpec((tm, d), lambda i, j: (i, 0)),
            pl.BlockSpec((1, d), lambda i, j: (0, 0)),
            pl.BlockSpec((d, tn), lambda i, j: (0, j)),
            pl.BlockSpec((d, LANES), lambda i, j: (0, 0)),
            pl.BlockSpec((1, LANES), lambda i, j: (0, 0)),
        ],
        out_specs=[
            pl.BlockSpec((tm, tn), lambda i, j: (i, j)),
            pl.BlockSpec((tm, LANES), lambda i, j: (i, 0)),
        ],
        out_shape=[
            jax.ShapeDtypeStruct((m, n), F32),
            jax.ShapeDtypeStruct((m, LANES), F32),
        ],
        scratch_shapes=[pltpu.VMEM((tm, d), BF16)],
        compiler_params=_params("parallel", "arbitrary"),
        name="in_proj",
    )(x, pre_g, w_cat, w_f, f_bias)


def _cumsum_kernel(lf_ref, c_ref, carry):
    @pl.when(pl.program_id(1) == 0)
    def _():
        carry[...] = jnp.zeros_like(carry)

    t = lf_ref.shape[0]
    row = lax.broadcasted_iota(jnp.int32, (t, t), 0)
    col = lax.broadcasted_iota(jnp.int32, (t, t), 1)
    tri = jnp.where(col <= row, 1.0, 0.0).astype(BF16)
    c = _dot01(tri, lf_ref[...]) + carry[...]
    c_ref[...] = c
    carry[...] = c[t - 1:t, :]


def _seq_cumsum(lf):
    b, l, w = lf.shape
    t = _tile(l, 512)
    return pl.pallas_call(
        _cumsum_kernel,
        grid=(b, l // t),
        in_specs=[pl.BlockSpec((None, t, w), lambda i, j: (i, j, 0))],
        out_specs=pl.BlockSpec((None, t, w), lambda i, j: (i, j, 0)),
        out_shape=jax.ShapeDtypeStruct((b, l, w), F32),
        scratch_shapes=[pltpu.VMEM((1, w), F32)],
        compiler_params=_params("parallel", "arbitrary"),
        name="fox_cumsum",
    )(lf)


def _hgrn_masks(c):
    trow = lax.broadcasted_iota(jnp.int32, (c, LANES), 0)
    ti = lax.broadcasted_iota(jnp.int32, (c, c), 0)
    si = lax.broadcasted_iota(jnp.int32, (c, c), 1)
    tri = jnp.where(si <= ti, 1.0, 0.0).astype(BF16)
    levels = []
    b = 2
    while b <= c:
        half = b // 2
        second = (trow & (b - 1)) >= half
        shift = b.bit_length() - 1
        pair = ((ti >> shift) == (si >> shift)) & ((ti & (b - 1)) >= half) & ((si & (b - 1)) < half)
        levels.append((b, second, pair))
        b *= 2
    return dict(trow=trow, tri=tri, eye=(ti == si), levels=levels)


def _hgrn_ref_rows(g, b, trow):
    c = g.shape[0]
    if b == 2:
        return jnp.where((trow & 1) == 1, pltpu.roll(g, 1, 0), g)
    if b == 4:
        m = trow & 3
        return jnp.where(m == 0, pltpu.roll(g, c - 1, 0),
                         jnp.where(m == 1, g,
                                   jnp.where(m == 2, pltpu.roll(g, 1, 0), pltpu.roll(g, 2, 0))))
    half = b // 2
    g3 = g.reshape(c // b, b, LANES)
    ref = jnp.broadcast_to(g3[:, half - 1:half, :], (c // b, b, LANES))
    return ref.reshape(c, LANES)


def _hgrn_chunk(q, x, v, lb, st, masks):
    c = q.shape[0]
    one_m_lb = 1.0 - lb
    logf = jnp.log(lb + one_m_lb * _sigmoid(x))
    kk = one_m_lb * _sigmoid(-x)
    qf = q * _sigmoid(q)
    g = _dot01(masks["tri"], logf)
    g_end = g[c - 1:c, :]
    vb = v.astype(BF16)

    o = _dot_nt((qf * jnp.exp(g)).astype(BF16), st.astype(BF16))
    a = jnp.where(masks["eye"], _dot_nt(qf.astype(BF16), kk.astype(BF16)), 0.0)
    for b, second, pair in masks["levels"]:
        e = jnp.exp(-jnp.abs(g - _hgrn_ref_rows(g, b, masks["trow"])))
        qb = jnp.where(second, qf * e, 0.0).astype(BF16)
        kb = jnp.where(second, 0.0, kk * e).astype(BF16)
        a = jnp.where(pair, _dot_nt(qb, kb), a)
    o = o + _dot(a.astype(BF16), vb)
    ke = (kk * jnp.exp(g_end - g)).astype(BF16)
    st_new = st * jnp.exp(g_end) + _dot_tn(vb, ke)
    return o, st_new


def _hgrn_kernel(q_ref, f_ref, i_ref, lb_ref, s0_ref, o_ref, sfin_ref, st_scr, *, chunk, heads):
    step = pl.program_id(1)
    n_chunks = q_ref.shape[0] // chunk

    @pl.when(step == 0)
    def _():
        for h in range(heads):
            st_scr[h] = s0_ref[h].T

    masks = _hgrn_masks(chunk)

    def body(ci, carry):
        rows = pl.ds(pl.multiple_of(ci * chunk, chunk), chunk)
        for h in range(heads):
            cols = slice(h * LANES, (h + 1) * LANES)
            o, st_new = _hgrn_chunk(q_ref[rows, cols], f_ref[rows, cols], i_ref[rows, cols],
                                    lb_ref[h:h + 1, :], st_scr[h], masks)
            o_ref[rows, cols] = o
            st_scr[h] = st_new
        return carry

    lax.fori_loop(0, n_chunks, body, 0)

    @pl.when(step == pl.num_programs(1) - 1)
    def _():
        for h in range(heads):
            sfin_ref[h] = st_scr[h].T


def _hgrn(p3, lb, s0, *, chunk, tb_pref):
    bsz, l, _ = p3.shape
    heads, dk = lb.shape
    w = heads * dk
    tb = _tile(l, tb_pref)
    kern = functools.partial(_hgrn_kernel, chunk=chunk, heads=heads)
    return pl.pallas_call(
        kern,
        grid=(bsz, l // tb),
        in_specs=[
            pl.BlockSpec((None, tb, w), lambda b, t: (b, t, 0)),
            pl.BlockSpec((None, tb, w), lambda b, t: (b, t, 1)),
            pl.BlockSpec((None, tb, w), lambda b, t: (b, t, 2)),
            pl.BlockSpec((heads, dk), lambda b, t: (0, 0)),
            pl.BlockSpec((None, heads, dk, dk), lambda b, t: (b, 0, 0, 0)),
        ],
        out_specs=[
            pl.BlockSpec((None, tb, w), lambda b, t: (b, t, 0)),
            pl.BlockSpec((None, heads, dk, dk), lambda b, t: (b, 0, 0, 0)),
        ],
        out_shape=[
            jax.ShapeDtypeStruct((bsz, l, w), F32),
            jax.ShapeDtypeStruct(s0.shape, F32),
        ],
        scratch_shapes=[pltpu.VMEM((heads, dk, dk), F32)],
        compiler_params=_params("parallel", "arbitrary"),
        name="hgrn2",
    )(p3, p3, p3, lb, s0)


def _fox_prompt_kernel(q_ref, k_ref, v_ref, ccol_ref, crow_ref, o_ref, m_scr, l_scr, acc_scr, *, scale):
    qi = pl.program_id(2)
    ki = pl.program_id(3)

    @pl.when(ki == 0)
    def _():
        m_scr[...] = jnp.full_like(m_scr, -jnp.inf)
        l_scr[...] = jnp.zeros_like(l_scr)
        acc_scr[...] = jnp.zeros_like(acc_scr)

    @pl.when(ki <= qi)
    def _():
        tq, tk = q_ref.shape[0], k_ref.shape[0]
        s = _dot_nt(q_ref[...].astype(BF16), k_ref[...].astype(BF16)) * scale
        s = s + ccol_ref[...] - crow_ref[...]
        row = lax.broadcasted_iota(jnp.int32, (tq, tk), 0)
        col = lax.broadcasted_iota(jnp.int32, (tq, tk), 1)
        s = jnp.where(col <= row + (qi - ki) * tq, s, -jnp.inf)
        m_old = m_scr[...]
        m_new = jnp.maximum(m_old, jnp.max(s, axis=-1, keepdims=True))
        alpha = jnp.exp(m_old - m_new)
        p = jnp.exp(s - m_new)
        l_scr[...] = alpha * l_scr[...] + jnp.sum(p, axis=-1, keepdims=True)
        acc_scr[...] = alpha * acc_scr[...] + _dot(p.astype(BF16), v_ref[...].astype(BF16))
        m_scr[...] = m_new

    @pl.when(ki == qi)
    def _():
        o_ref[...] = acc_scr[...] / l_scr[...]


def _fox_prompt(p3, ccol, crow, *, heads, dh, col0):
    bsz, l, _ = p3.shape
    t = _tile(l, 512)
    nq = l // t
    kern = functools.partial(_fox_prompt_kernel, scale=dh ** -0.5)
    return pl.pallas_call(
        kern,
        grid=(bsz, heads, nq, nq),
        in_specs=[
            pl.BlockSpec((None, t, dh), lambda b, h, qi, ki: (b, qi, col0 + h)),
            pl.BlockSpec((None, t, dh), lambda b, h, qi, ki: (b, jnp.minimum(ki, qi), col0 + heads + h)),
            pl.BlockSpec((None, t, dh), lambda b, h, qi, ki: (b, jnp.minimum(ki, qi), col0 + 2 * heads + h)),
            pl.BlockSpec((None, None, t, 1), lambda b, h, qi, ki: (b, h, qi, 0)),
            pl.BlockSpec((None, None, 1, t), lambda b, h, qi, ki: (b, h, 0, jnp.minimum(ki, qi))),
        ],
        out_specs=pl.BlockSpec((None, t, dh), lambda b, h, qi, ki: (b, qi, h)),
        out_shape=jax.ShapeDtypeStruct((bsz, l, heads * dh), F32),
        scratch_shapes=[pltpu.VMEM((t, 1), F32), pltpu.VMEM((t, 1), F32), pltpu.VMEM((t, dh), F32)],
        compiler_params=_params("parallel", "parallel", "parallel", "arbitrary"),
        name="fox_prompt",
    )(p3, p3, p3, ccol, crow)


def _suffix_kernel(pt_ref, lf_ref, r_ref, carry):
    @pl.when(pl.program_id(1) == 0)
    def _():
        carry[...] = jnp.zeros_like(carry)

    lf = lf_ref[...]
    n = lf.shape[0]
    row = lax.broadcasted_iota(jnp.int32, (n, n), 0)
    col = lax.broadcasted_iota(jnp.int32, (n, n), 1)
    upper = jnp.where(col > row, 1.0, 0.0).astype(BF16)
    r_ref[...] = _dot01(upper, lf) + carry[...]
    carry[...] = carry[...] + jnp.sum(lf, axis=0, keepdims=True)


def _past_suffix(cache_logf, page_table, layer):
    _, _, rows, heads = cache_logf.shape
    bd, n_pages = page_table.shape
    return pl.pallas_call(
        _suffix_kernel,
        grid_spec=pltpu.PrefetchScalarGridSpec(
            num_scalar_prefetch=1,
            grid=(bd, n_pages),
            in_specs=[pl.BlockSpec((None, None, rows, heads),
                                   lambda b, g, pt: (layer, pt[b, n_pages - 1 - g], 0, 0))],
            out_specs=pl.BlockSpec((None, None, rows, heads),
                                   lambda b, g, pt: (b, n_pages - 1 - g, 0, 0)),
            scratch_shapes=[pltpu.VMEM((1, heads), F32)],
        ),
        out_shape=jax.ShapeDtypeStruct((bd, n_pages, rows, heads), F32),
        compiler_params=_params("parallel", "arbitrary"),
        name="fox_past_suffix",
    )(page_table, cache_logf)


def _fox_sample_kernel(pt_ref, q_ref, kn_ref, vn_ref, lfr_ref, lfc_ref, k_ref, v_ref, r_ref, o_ref,
                       m_scr, l_scr, acc_scr, cnc_scr, *, scale, heads):
    g = pl.program_id(1)
    n = q_ref.shape[0]
    qb = q_ref[...].astype(BF16)

    @pl.when(g == 0)
    def _():
        ii = lax.broadcasted_iota(jnp.int32, (n, n), 0)
        jj = lax.broadcasted_iota(jnp.int32, (n, n), 1)
        same_head = (ii & (heads - 1)) == (jj & (heads - 1))
        m_row = jnp.where(same_head & (ii <= jj), 1.0, 0.0).astype(BF16)
        m_col = jnp.where(same_head & (jj <= ii), 1.0, 0.0).astype(BF16)
        cn_row = _dot01_right(jnp.broadcast_to(lfr_ref[...], (8, n)), m_row)[0:1, :]
        cn_col = _dot01(m_col, lfc_ref[...])
        cnc_scr[...] = cn_col
        s = _dot_nt(qb, kn_ref[...].astype(BF16)) * scale + cn_col - cn_row
        s = jnp.where(same_head & (jj <= ii), s, -jnp.inf)
        m0 = jnp.max(s, axis=-1, keepdims=True)
        p = jnp.exp(s - m0)
        m_scr[...] = m0
        l_scr[...] = jnp.sum(p, axis=-1, keepdims=True)
        acc_scr[...] = _dot(p.astype(BF16), vn_ref[...].astype(BF16))

    w = k_ref.shape[0]
    ii = lax.broadcasted_iota(jnp.int32, (n, w), 0)
    jj = lax.broadcasted_iota(jnp.int32, (n, w), 1)
    s = _dot_nt(qb, k_ref[...].astype(BF16)) * scale + cnc_scr[...] + r_ref[...]
    s = jnp.where((ii & (heads - 1)) == (jj & (heads - 1)), s, -jnp.inf)
    m_old = m_scr[...]
    m_new = jnp.maximum(m_old, jnp.max(s, axis=-1, keepdims=True))
    alpha = jnp.exp(m_old - m_new)
    p = jnp.exp(s - m_new)
    l_scr[...] = alpha * l_scr[...] + jnp.sum(p, axis=-1, keepdims=True)
    acc_scr[...] = alpha * acc_scr[...] + _dot(p.astype(BF16), v_ref[...].astype(BF16))
    m_scr[...] = m_new

    @pl.when(g == pl.num_programs(1) - 1)
    def _():
        o_ref[...] = acc_scr[...] / l_scr[...]


def _fox_sample(q, k_new, v_new, lf_row, lf_col, cache_k, cache_v, r_bias, page_table, layer, *, heads):
    bd, n, dh = q.shape
    w = cache_k.shape[2]
    n_pages = page_table.shape[1]
    kern = functools.partial(_fox_sample_kernel, scale=dh ** -0.5, heads=heads)
    tok = lambda b, g, pt: (b, 0, 0)
    page = lambda b, g, pt: (layer, pt[b, g], 0, 0)
    return pl.pallas_call(
        kern,
        grid_spec=pltpu.PrefetchScalarGridSpec(
            num_scalar_prefetch=1,
            grid=(bd, n_pages),
            in_specs=[
                pl.BlockSpec((None, n, dh), tok),
                pl.BlockSpec((None, n, dh), tok),
                pl.BlockSpec((None, n, dh), tok),
                pl.BlockSpec((None, 1, n), tok),
                pl.BlockSpec((None, n, 1), tok),
                pl.BlockSpec((None, None, w, dh), page),
                pl.BlockSpec((None, None, w, dh), page),
                pl.BlockSpec((None, None, 1, w), lambda b, g, pt: (b, g, 0, 0)),
            ],
            out_specs=pl.BlockSpec((None, n, dh), tok),
            scratch_shapes=[pltpu.VMEM((n, 1), F32), pltpu.VMEM((n, 1), F32),
                            pltpu.VMEM((n, dh), F32), pltpu.VMEM((n, 1), F32)],
        ),
        out_shape=jax.ShapeDtypeStruct((bd, n, dh), F32),
        compiler_params=_params("parallel", "arbitrary"),
        name="fox_sample",
    )(page_table, q, k_new, v_new, lf_row, lf_col, cache_k, cache_v, r_bias)


def _merge_kernel(x_ref, oa_ref, ob_ref, za_ref, zb_ref, ga_ref, gb_ref, ng_ref, wa_ref, wb_ref, wo_ref,
                  pg_ref, y_ref, *, heads):
    ng = ng_ref[...]
    parts = []
    for h in range(heads):
        cols = slice(h * LANES, (h + 1) * LANES)
        blk = oa_ref[:, cols]
        nrm = blk * lax.rsqrt(jnp.mean(blk * blk, axis=-1, keepdims=True) + RMS_EPS) * ng
        za = za_ref[:, cols]
        parts.append((nrm * (za * _sigmoid(za))).astype(BF16))
    ya = jnp.concatenate(parts, axis=-1)
    zb = zb_ref[...]
    yb = (ob_ref[...] * (zb * _sigmoid(zb))).astype(BF16)
    u = _sigmoid(ga_ref[...]) * _dot(ya, wa_ref[...]) + _sigmoid(gb_ref[...]) * _dot(yb, wb_ref[...])
    mix = _dot(u.astype(BF16), wo_ref[...])
    nrm = mix * lax.rsqrt(jnp.mean(mix * mix, axis=-1, keepdims=True) + RMS_EPS) * pg_ref[...]
    y_ref[...] = x_ref[...] + nrm


def _merge(x, oa, ob, p, norm_g, w_a, w_b, w_out, post_g, *, heads, tm_pref):
    m, d = x.shape
    w = oa.shape[1]
    tm = _tile(m, tm_pref)
    nw = w
    g_col = (8 * nw) // d
    const = lambda i: (0, 0)
    return pl.pallas_call(
        functools.partial(_merge_kernel, heads=heads),
        grid=(m // tm,),
        in_specs=[
            pl.BlockSpec((tm, d), lambda i: (i, 0)),
            pl.BlockSpec((tm, w), lambda i: (i, 0)),
            pl.BlockSpec((tm, w), lambda i: (i, 0)),
            pl.BlockSpec((tm, nw), lambda i: (i, 3)),
            pl.BlockSpec((tm, nw), lambda i: (i, 7)),
            pl.BlockSpec((tm, d), lambda i: (i, g_col)),
            pl.BlockSpec((tm, d), lambda i: (i, g_col + 1)),
            pl.BlockSpec((1, LANES), const),
            pl.BlockSpec(w_a.shape, const),
            pl.BlockSpec(w_b.shape, const),
            pl.BlockSpec(w_out.shape, const),
            pl.BlockSpec((1, d), const),
        ],
        out_specs=pl.BlockSpec((tm, d), lambda i: (i, 0)),
        out_shape=jax.ShapeDtypeStruct((m, d), F32),
        compiler_params=_params("parallel"),
        name="merge_out",
    )(x, oa, ob, p, p, p, p, norm_g, w_a, w_b, w_out, post_g)


def kernel(x_prompt, x_sample, state_hgrn, cache_k, cache_v, cache_logf, page_table, w_in, hgrn_lower_bounds,
           hgrn_norm_g, fox_f_bias, w_branch_a, w_branch_b, w_out, pre_norm_g, post_norm_g):
    bp, seq, d = x_prompt.shape
    bd, tdec, _ = x_sample.shape
    depth, _, heads, dk, dv = state_hgrn.shape
    _, n_pool, page, hb, dh = cache_k.shape
    assert heads == hb and dk == dv == dh == LANES and (8 * heads * dk) % d == 0
    w = heads * dk
    n_pages = page_table.shape[1]

    lb_all = _lower_bounds(hgrn_lower_bounds).reshape(depth, heads, dk)

    w_cat = jnp.concatenate([w_in[:, :, :8 * w], w_in[:, :, 8 * w + heads:]], axis=-1).astype(BF16)
    w_f = jnp.pad(w_in[:, :, 8 * w:8 * w + heads], ((0, 0), (0, 0), (0, LANES - heads))).astype(BF16)
    f_bias = jnp.pad(fox_f_bias.astype(F32), ((0, 0), (0, LANES - heads)))[:, None, :]
    w_a16 = w_branch_a.astype(BF16)
    w_b16 = w_branch_b.astype(BF16)
    w_o16 = w_out.astype(BF16)
    pre_g = pre_norm_g.astype(F32)[:, None, :]
    post_g = post_norm_g.astype(F32)[:, None, :]
    norm_g = hgrn_norm_g.astype(F32)[:, None, :]

    ck = cache_k.reshape(depth, n_pool, page * heads, dh)
    cv = cache_v.reshape(depth, n_pool, page * heads, dh)

    xp = x_prompt.reshape(bp * seq, d)
    xs = x_sample.reshape(bd * tdec, d)
    zeros_state = jnp.zeros((bp, heads, dk, dv), F32)
    c_prompt = HGRN_CHUNK if seq % HGRN_CHUNK == 0 else 1
    c_sample = tdec
    assert c_prompt > 1 and (c_sample & (c_sample - 1)) == 0

    st_p, st_s, kp_l, vp_l, lfp_l, ks_l, vs_l, lfs_l = [], [], [], [], [], [], [], []
    for l in range(depth):
        p, lf = _in_proj(xp, pre_g[l], w_cat[l], w_f[l], f_bias[l], tm_pref=512)
        p3 = p.reshape(bp, seq, -1)
        oa, s_fin = _hgrn(p3, lb_all[l], zeros_state, chunk=c_prompt, tb_pref=512)
        c8 = _seq_cumsum(lf.reshape(bp, seq, LANES))[:, :, :heads].transpose(0, 2, 1)
        ob = _fox_prompt(p3, c8[:, :, :, None], c8[:, :, None, :], heads=heads, dh=dh, col0=4 * heads)
        xp = _merge(xp, oa.reshape(bp * seq, w), ob.reshape(bp * seq, w), p, norm_g[l], w_a16[l], w_b16[l],
                    w_o16[l], post_g[l], heads=heads, tm_pref=256)
        st_p.append(s_fin)
        kp_l.append(p[:, 5 * w:6 * w].reshape(bp, seq, heads, dh))
        vp_l.append(p[:, 6 * w:7 * w].reshape(bp, seq, heads, dh))
        lfp_l.append(lf[:, :heads].reshape(bp, seq, heads))

        p, lf = _in_proj(xs, pre_g[l], w_cat[l], w_f[l], f_bias[l], tm_pref=512)
        p3 = p.reshape(bd, tdec, -1)
        oa, s_fin = _hgrn(p3, lb_all[l], state_hgrn[l].astype(F32), chunk=c_sample, tb_pref=tdec)
        lf8 = lf[:, :heads].reshape(bd, tdec * heads)
        r_bias = _past_suffix(cache_logf, page_table, l).reshape(bd, n_pages, 1, page * heads)
        nh = tdec * heads
        ob = _fox_sample(p[:, 4 * w:5 * w].reshape(bd, nh, dh), p[:, 5 * w:6 * w].reshape(bd, nh, dh),
                         p[:, 6 * w:7 * w].reshape(bd, nh, dh), lf8[:, None, :], lf8[:, :, None],
                         ck, cv, r_bias, page_table, l, heads=heads)
        xs = _merge(xs, oa.reshape(bd * tdec, w), ob.reshape(bd * tdec, w), p, norm_g[l], w_a16[l], w_b16[l],
                    w_o16[l], post_g[l], heads=heads, tm_pref=256)
        st_s.append(s_fin)
        ks_l.append(p[:, 5 * w:6 * w].reshape(bd, tdec, heads, dh))
        vs_l.append(p[:, 6 * w:7 * w].reshape(bd, tdec, heads, dh))
        lfs_l.append(lf[:, :heads].reshape(bd, tdec, heads))

    return (xp.reshape(bp, seq, d), xs.reshape(bd, tdec, d), jnp.stack(st_p), jnp.stack(st_s),
            jnp.stack(kp_l), jnp.stack(vp_l), jnp.stack(lfp_l), jnp.stack(ks_l), jnp.stack(vs_l),
            jnp.stack(lfs_l))
```

```python
import functools

import jax
import jax.numpy as jnp
from jax import lax
from jax.experimental import pallas as pl
from jax.experimental.pallas import tpu as pltpu

F32 = jnp.float32
BF16 = jnp.bfloat16

RMS_EPS = 1e-6
LOG2E = 1.4426950408889634
HGRN_CHUNK = 64
LANES = 128
TM_PROJ = 512
PAGES_PER_STEP = 8
FOX_TILE = 512
FOX_SUB_ROWS = 128
VMEM_LIMIT_BYTES = 60000 * 1024


def _params(*sem):
    return pltpu.CompilerParams(dimension_semantics=sem, vmem_limit_bytes=VMEM_LIMIT_BYTES)


def _tile(n, pref):
    if n <= pref:
        return n
    t = pref
    while n % t:
        t //= 2
    return t


def _dot(a, b):
    return jnp.dot(a, b, preferred_element_type=F32)


def _dot_nt(a, b):
    return lax.dot_general(a, b, (((1,), (1,)), ((), ())), preferred_element_type=F32)


def _dot_tn(a, b):
    return lax.dot_general(a, b, (((0,), (0,)), ((), ())), preferred_element_type=F32)


def _split3(a):
    hi = a.astype(BF16)
    r = a - hi.astype(F32)
    mid = r.astype(BF16)
    lo = (r - mid.astype(F32)).astype(BF16)
    return hi, mid, lo


def _dot01(m01, a):
    hi, mid, lo = _split3(a)
    return _dot(m01, hi) + _dot(m01, mid) + _dot(m01, lo)


def _dot01_right(a, m01):
    hi, mid, lo = _split3(a)
    return _dot(hi, m01) + _dot(mid, m01) + _dot(lo, m01)


def _sigmoid(x):
    return 1.0 / (1.0 + jnp.exp(-x))


def _log_sigmoid(z):
    return jnp.minimum(z, 0.0) - jnp.log(1.0 + jnp.exp(-jnp.abs(z)))


def _lb_kernel(x_ref, o_ref):
    x = x_ref[...]
    depth = x.shape[0]
    e = jnp.exp(x - jnp.max(x, axis=0, keepdims=True))
    sm = e / jnp.sum(e, axis=0, keepdims=True)
    acc = jnp.zeros_like(sm[0:1])
    for i in range(depth):
        o_ref[i:i + 1, :] = acc
        if i + 1 < depth:
            acc = acc + sm[i + 1:i + 2]


def _lower_bounds(lb_param):
    return pl.pallas_call(
        _lb_kernel,
        out_shape=jax.ShapeDtypeStruct(lb_param.shape, F32),
        name="hgrn_lower_bounds",
    )(lb_param.astype(F32))


QB_GROUP, KB_GROUP, VB_GROUP = 4, 5, 6


def _inproj_kernel(x_ref, g_ref, w_ref, wf_ref, fb_ref, kst_ref, vst_ref, p_ref, lf_ref, qkv16_ref, k_ref, v_ref,
                   h_scr, *, q_scale):
    del kst_ref, vst_ref
    j = pl.program_id(1)

    @pl.when(j == 0)
    def _():
        x = x_ref[...]
        h = x * lax.rsqrt(jnp.mean(x * x, axis=-1, keepdims=True) + RMS_EPS) * g_ref[...]
        hb = h.astype(BF16)
        h_scr[...] = hb
        lf_ref[...] = _log_sigmoid(_dot(hb, wf_ref[...]) + fb_ref[...])

    res = _dot(h_scr[...], w_ref[...])
    p_ref[...] = res

    @pl.when(j == QB_GROUP)
    def _():
        qkv16_ref[...] = (res * q_scale).astype(BF16)

    @pl.when(j == KB_GROUP)
    def _():
        qkv16_ref[...] = res.astype(BF16)
        k_ref[...] = res

    @pl.when(j == VB_GROUP)
    def _():
        qkv16_ref[...] = res.astype(BF16)
        v_ref[...] = res


def _in_proj(x, pre_g, w_cat, w_f, f_bias, k_stack, v_stack, layer, *, group_w, q_scale, tm_pref):
    m, d = x.shape
    n = w_cat.shape[1]
    tm = _tile(m, tm_pref)
    tn = group_w
    assert n % tn == 0
    return pl.pallas_call(
        functools.partial(_inproj_kernel, q_scale=q_scale),
        grid=(m // tm, n // tn),
        in_specs=[
            pl.BlockSpec((tm, d), lambda i, j: (i, 0)),
            pl.BlockSpec((1, d), lambda i, j: (0, 0)),
            pl.BlockSpec((d, tn), lambda i, j: (0, j)),
            pl.BlockSpec((d, LANES), lambda i, j: (0, 0)),
            pl.BlockSpec((1, LANES), lambda i, j: (0, 0)),
            pl.BlockSpec(memory_space=pl.ANY),
            pl.BlockSpec(memory_space=pl.ANY),
        ],
        out_specs=[
            pl.BlockSpec((tm, tn), lambda i, j: (i, j)),
            pl.BlockSpec((tm, LANES), lambda i, j: (i, 0)),
            pl.BlockSpec((tm, tn), lambda i, j: (i, jnp.clip(j - QB_GROUP, 0, 2))),
            pl.BlockSpec((None, tm, tn), lambda i, j: (layer, i, 0)),
            pl.BlockSpec((None, tm, tn), lambda i, j: (layer, i, 0)),
        ],
        out_shape=[
            jax.ShapeDtypeStruct((m, n), F32),
            jax.ShapeDtypeStruct((m, LANES), F32),
            jax.ShapeDtypeStruct((m, 3 * tn), BF16),
            jax.ShapeDtypeStruct(k_stack.shape, F32),
            jax.ShapeDtypeStruct(v_stack.shape, F32),
        ],
        input_output_aliases={5: 3, 6: 4},
        scratch_shapes=[pltpu.VMEM((tm, d), BF16)],
        compiler_params=_params("parallel", "arbitrary"),
        name="in_proj",
    )(x, pre_g, w_cat, w_f, f_bias, k_stack, v_stack)


def _cumsum_kernel(lf_ref, c_ref, ct_ref, carry):
    @pl.when(pl.program_id(1) == 0)
    def _():
        carry[...] = jnp.zeros_like(carry)

    t = lf_ref.shape[0]
    row = lax.broadcasted_iota(jnp.int32, (t, t), 0)
    col = lax.broadcasted_iota(jnp.int32, (t, t), 1)
    tri = jnp.where(col <= row, 1.0, 0.0).astype(BF16)
    c_nat = _dot01(tri, lf_ref[...]) + carry[...]
    carry[...] = c_nat[t - 1:t, :]
    c = c_nat * LOG2E
    heads = ct_ref.shape[0]
    for h in range(heads):
        c_ref[h] = jnp.broadcast_to(c[:, h:h + 1], c.shape)
    ct_ref[...] = c.T[:heads, :]


def _seq_cumsum(lf, heads):
    b, l, w = lf.shape
    t = _tile(l, 512)
    return pl.pallas_call(
        _cumsum_kernel,
        grid=(b, l // t),
        in_specs=[pl.BlockSpec((None, t, w), lambda i, j: (i, j, 0))],
        out_specs=[pl.BlockSpec((None, heads, t, w), lambda i, j: (i, 0, j, 0)),
                   pl.BlockSpec((None, heads, t), lambda i, j: (i, 0, j))],
        out_shape=[jax.ShapeDtypeStruct((b, heads, l, w), F32), jax.ShapeDtypeStruct((b, heads, l), F32)],
        scratch_shapes=[pltpu.VMEM((1, w), F32)],
        compiler_params=_params("parallel", "arbitrary"),
        name="fox_cumsum",
    )(lf)


def _hgrn_masks(c):
    trow = lax.broadcasted_iota(jnp.int32, (c, LANES), 0)
    ti = lax.broadcasted_iota(jnp.int32, (c, c), 0)
    si = lax.broadcasted_iota(jnp.int32, (c, c), 1)
    tri = jnp.where(si <= ti, 1.0, 0.0).astype(BF16)
    levels = []
    b = 2
    while b <= c:
        half = b // 2
        second = (trow & (b - 1)) >= half
        shift = b.bit_length() - 1
        pair = ((ti >> shift) == (si >> shift)) & ((ti & (b - 1)) >= half) & ((si & (b - 1)) < half)
        levels.append((b, second, pair))
        b *= 2
    return dict(trow=trow, tri=tri, eye=(ti == si), levels=levels)


def _hgrn_ref_rows(g, b, trow):
    c = g.shape[0]
    if b == 2:
        return jnp.where((trow & 1) == 1, pltpu.roll(g, 1, 0), g)
    if b == 4:
        m = trow & 3
        return jnp.where(m == 0, pltpu.roll(g, c - 1, 0),
                         jnp.where(m == 1, g,
                                   jnp.where(m == 2, pltpu.roll(g, 1, 0), pltpu.roll(g, 2, 0))))
    half = b // 2
    g3 = g.reshape(c // b, b, LANES)
    ref = jnp.broadcast_to(g3[:, half - 1:half, :], (c // b, b, LANES))
    return ref.reshape(c, LANES)


def _hgrn_chunk(q, x, v, lb, st, masks):
    c = q.shape[0]
    one_m_lb = 1.0 - lb
    logf = jnp.log(lb + one_m_lb * _sigmoid(x))
    kk = one_m_lb * _sigmoid(-x)
    qf = q * _sigmoid(q)
    g = _dot01(masks["tri"], logf)
    g_end = g[c - 1:c, :]
    vb = v.astype(BF16)

    o = _dot_nt((qf * jnp.exp(g)).astype(BF16), st.astype(BF16))
    a = jnp.where(masks["eye"], _dot_nt(qf.astype(BF16), kk.astype(BF16)), 0.0)
    for b, second, pair in masks["levels"]:
        e = jnp.exp(-jnp.abs(g - _hgrn_ref_rows(g, b, masks["trow"])))
        qb = jnp.where(second, qf * e, 0.0).astype(BF16)
        kb = jnp.where(second, 0.0, kk * e).astype(BF16)
        a = jnp.where(pair, _dot_nt(qb, kb), a)
    o = o + _dot(a.astype(BF16), vb)
    ke = (kk * jnp.exp(g_end - g)).astype(BF16)
    st_new = st * jnp.exp(g_end) + _dot_tn(vb, ke)
    return o, st_new


def _hgrn_kernel(q_ref, f_ref, i_ref, lb_ref, s0_ref, o_ref, sfin_ref, st_scr, *, chunk, heads):
    step = pl.program_id(1)
    n_chunks = q_ref.shape[0] // chunk

    @pl.when(step == 0)
    def _():
        for h in range(heads):
            st_scr[h] = s0_ref[h].T

    masks = _hgrn_masks(chunk)

    def body(ci, carry):
        rows = pl.ds(pl.multiple_of(ci * chunk, chunk), chunk)
        for h in range(heads):
            cols = slice(h * LANES, (h + 1) * LANES)
            o, st_new = _hgrn_chunk(q_ref[rows, cols], f_ref[rows, cols], i_ref[rows, cols],
                                    lb_ref[h:h + 1, :], st_scr[h], masks)
            o_ref[rows, cols] = o
            st_scr[h] = st_new
        return carry

    lax.fori_loop(0, n_chunks, body, 0)

    @pl.when(step == pl.num_programs(1) - 1)
    def _():
        for h in range(heads):
            sfin_ref[h] = st_scr[h].T


def _hgrn(p3, lb, s0, *, chunk, tb_pref):
    bsz, l, _ = p3.shape
    heads, dk = lb.shape
    w = heads * dk
    tb = _tile(l, tb_pref)
    kern = functools.partial(_hgrn_kernel, chunk=chunk, heads=heads)
    return pl.pallas_call(
        kern,
        grid=(bsz, l // tb),
        in_specs=[
            pl.BlockSpec((None, tb, w), lambda b, t: (b, t, 0)),
            pl.BlockSpec((None, tb, w), lambda b, t: (b, t, 1)),
            pl.BlockSpec((None, tb, w), lambda b, t: (b, t, 2)),
            pl.BlockSpec((heads, dk), lambda b, t: (0, 0)),
            pl.BlockSpec((None, heads, dk, dk), lambda b, t: (b, 0, 0, 0)),
        ],
        out_specs=[
            pl.BlockSpec((None, tb, w), lambda b, t: (b, t, 0)),
            pl.BlockSpec((None, heads, dk, dk), lambda b, t: (b, 0, 0, 0)),
        ],
        out_shape=[
            jax.ShapeDtypeStruct((bsz, l, w), F32),
            jax.ShapeDtypeStruct(s0.shape, F32),
        ],
        scratch_shapes=[pltpu.VMEM((heads, dk, dk), F32)],
        compiler_params=_params("parallel", "arbitrary"),
        name="hgrn2",
    )(p3, p3, p3, lb, s0)


def _fox_prompt_kernel(qi_tab, ki_tab, q_ref, k_ref, v_ref, ccol_ref, crow_ref, o_ref, m_scr, l_scr, acc_scr,
                       *, heads, dh):
    step = pl.program_id(1)
    qi = qi_tab[step]
    ki = ki_tab[step]
    tq, tk = q_ref.shape[0], k_ref.shape[0]

    @pl.when(ki == 0)
    def _():
        m_scr[...] = jnp.full_like(m_scr, -jnp.inf)
        l_scr[...] = jnp.zeros_like(l_scr)
        acc_scr[...] = jnp.zeros_like(acc_scr)

    def block(diagonal):
        sub = min(tq, FOX_SUB_ROWS)
        units = [(h, r0) for h in range(heads) for r0 in range(0, tq, sub)]

        def scores(h, r0):
            kw = r0 + sub if diagonal else tk
            return _dot_nt(q_ref[r0:r0 + sub, h * dh:(h + 1) * dh], k_ref[:kw, h * dh:(h + 1) * dh])

        s_next = scores(*units[0])
        for u, (h, r0) in enumerate(units):
            s = s_next
            if u + 1 < len(units):
                s_next = scores(*units[u + 1])
            rows = slice(r0, r0 + sub)
            cols = slice(h * dh, (h + 1) * dh)
            kw = s.shape[1]
            s = s - crow_ref[h:h + 1, :kw]
            if diagonal:
                row = lax.broadcasted_iota(jnp.int32, (sub, kw), 0) + r0
                col = lax.broadcasted_iota(jnp.int32, (sub, kw), 1)
                s = jnp.where(col <= row, s, -jnp.inf)
            cc = ccol_ref[h, rows, :]
            m_old = m_scr[h, rows, :]
            m_new = jnp.maximum(m_old, jnp.max(s, axis=-1, keepdims=True) + cc)
            p = jnp.exp2(s - pltpu.repeat(m_new - cc, kw // LANES, axis=1))
            alpha = jnp.exp2(m_old - m_new)
            l_new = alpha * l_scr[h, rows, :] + jnp.sum(p, axis=-1, keepdims=True)
            acc = alpha * acc_scr[rows, cols] + _dot(p.astype(BF16), v_ref[:kw, cols])
            m_scr[h, rows, :] = m_new
            if diagonal:
                o_ref[rows, cols] = acc / l_new
            else:
                l_scr[h, rows, :] = l_new
                acc_scr[rows, cols] = acc

    @pl.when(ki < qi)
    def _():
        block(False)

    @pl.when(ki == qi)
    def _():
        block(True)


def _fox_prompt(qkv16, ccol, crow, *, heads, dh):
    assert dh == LANES
    bsz, l, _ = qkv16.shape
    w = heads * dh
    t = _tile(l, FOX_TILE)
    nq = l // t
    pairs = [(qi, ki) for qi in range(nq) for ki in range(qi + 1)]
    qi_tab = jnp.asarray([p[0] for p in pairs], jnp.int32)
    ki_tab = jnp.asarray([p[1] for p in pairs], jnp.int32)
    return pl.pallas_call(
        functools.partial(_fox_prompt_kernel, heads=heads, dh=dh),
        grid_spec=pltpu.PrefetchScalarGridSpec(
            num_scalar_prefetch=2,
            grid=(bsz, len(pairs)),
            in_specs=[
                pl.BlockSpec((None, t, w), lambda b, s, qt, kt: (b, qt[s], 0)),
                pl.BlockSpec((None, t, w), lambda b, s, qt, kt: (b, kt[s], 1)),
                pl.BlockSpec((None, t, w), lambda b, s, qt, kt: (b, kt[s], 2)),
                pl.BlockSpec((None, heads, t, LANES), lambda b, s, qt, kt: (b, 0, qt[s], 0)),
                pl.BlockSpec((None, heads, t), lambda b, s, qt, kt: (b, 0, kt[s])),
            ],
            out_specs=pl.BlockSpec((None, t, w), lambda b, s, qt, kt: (b, qt[s], 0)),
            scratch_shapes=[pltpu.VMEM((heads, t, LANES), F32), pltpu.VMEM((heads, t, LANES), F32),
                            pltpu.VMEM((t, w), F32)],
        ),
        out_shape=jax.ShapeDtypeStruct((bsz, l, w), F32),
        compiler_params=_params("parallel", "arbitrary"),
        name="fox_prompt",
    )(qi_tab, ki_tab, qkv16, qkv16, qkv16, ccol, crow)


def _fox_sample_kernel(pt_ref, q_ref, kn_ref, vn_ref, lfr_ref, lfc_ref, *refs, heads, pp):
    del pt_ref
    k_refs, v_refs, lf_refs = refs[:pp], refs[pp:2 * pp], refs[2 * pp:3 * pp]
    o_ref = refs[3 * pp]
    m_scr, l_scr, acc_scr, rb_scr, carry_scr, lf_scr = refs[3 * pp + 1:]
    g = pl.program_id(1)
    n = q_ref.shape[0]
    w = k_refs[0].shape[0]
    qb = q_ref[...]

    @pl.when(g == 0)
    def _():
        ii = lax.broadcasted_iota(jnp.int32, (n, n), 0)
        jj = lax.broadcasted_iota(jnp.int32, (n, n), 1)
        same_head = (ii & (heads - 1)) == (jj & (heads - 1))
        m_row = jnp.where(same_head & (ii <= jj), 1.0, 0.0).astype(BF16)
        m_col = jnp.where(same_head & (jj <= ii), 1.0, 0.0).astype(BF16)
        cn_row = _dot01_right(jnp.broadcast_to(lfr_ref[...], (8, n)), m_row)[0:1, :] * LOG2E
        cn_col = _dot01(m_col, lfc_ref[...]) * LOG2E
        iw = lax.broadcasted_iota(jnp.int32, (n, w), 0)
        jw = lax.broadcasted_iota(jnp.int32, (n, w), 1)
        rb_scr[...] = jnp.where((iw & (heads - 1)) == (jw & (heads - 1)), cn_col, -jnp.inf)
        carry_scr[...] = jnp.zeros_like(carry_scr)
        s = _dot_nt(qb, kn_ref[...]) + cn_col - cn_row
        s = jnp.where(same_head & (jj <= ii), s, -jnp.inf)
        m0 = jnp.max(s, axis=-1, keepdims=True)
        p = jnp.exp2(s - m0)
        m_scr[...] = m0
        l_scr[...] = jnp.sum(p, axis=-1, keepdims=True)
        acc_scr[...] = _dot(p.astype(BF16), vn_ref[...])

    for j in range(pp):
        lf_scr[j:j + 1, :] = lf_refs[j][...]
    x = lf_scr[...] * LOG2E
    lane = lax.broadcasted_iota(jnp.int32, (pp, w), 1)
    incl, tot = x, x
    d = heads
    while d < w:
        incl = incl + jnp.where(lane < w - d, pltpu.roll(incl, w - d, 1), 0.0)
        tot = tot + pltpu.roll(tot, d, 1)
        d *= 2
    excl = jnp.where(lane < w - heads, pltpu.roll(incl, w - heads, 1), 0.0)
    run = carry_scr[...]
    bias = [None] * pp
    for j in reversed(range(pp)):
        bias[j] = excl[j:j + 1, :] + run
        run = run + tot[j:j + 1, :]
    carry_scr[...] = run

    rb = rb_scr[...]
    s_list = [_dot_nt(qb, k_refs[j][...].astype(BF16)) + rb + bias[j] for j in range(pp)]
    mx = s_list[0]
    for s in s_list[1:]:
        mx = jnp.maximum(mx, s)
    m_old = m_scr[...]
    m_new = jnp.maximum(m_old, jnp.max(mx, axis=-1, keepdims=True))
    alpha = jnp.exp2(m_old - m_new)
    l_new = alpha * l_scr[...]
    acc = alpha * acc_scr[...]
    for j in range(pp):
        p = jnp.exp2(s_list[j] - m_new)
        l_new = l_new + jnp.sum(p, axis=-1, keepdims=True)
        acc = acc + _dot(p.astype(BF16), v_refs[j][...].astype(BF16))
    m_scr[...] = m_new
    l_scr[...] = l_new
    acc_scr[...] = acc

    @pl.when(g == pl.num_programs(1) - 1)
    def _():
        o_ref[...] = acc / l_new


def _fox_sample(q16, k16, v16, lf_row, lf_col, cache_k, cache_v, cache_lf, page_table, layer, *, heads, pp):
    bd, n, dh = q16.shape
    w = cache_k.shape[2]
    n_pages = page_table.shape[1]
    assert n_pages % pp == 0
    steps = n_pages // pp
    tok = lambda b, g, pt: (b, 0, 0)

    def page(j):
        return lambda b, g, pt: (layer, pt[b, n_pages - pp * (g + 1) + j], 0, 0)

    return pl.pallas_call(
        functools.partial(_fox_sample_kernel, heads=heads, pp=pp),
        grid_spec=pltpu.PrefetchScalarGridSpec(
            num_scalar_prefetch=1,
            grid=(bd, steps),
            in_specs=[
                pl.BlockSpec((None, n, dh), tok),
                pl.BlockSpec((None, n, dh), tok),
                pl.BlockSpec((None, n, dh), tok),
                pl.BlockSpec((None, 1, n), tok),
                pl.BlockSpec((None, n, 1), tok),
            ] + [pl.BlockSpec((None, None, w, dh), page(j)) for j in range(pp)]
              + [pl.BlockSpec((None, None, w, dh), page(j)) for j in range(pp)]
              + [pl.BlockSpec((None, None, 1, w), page(j)) for j in range(pp)],
            out_specs=pl.BlockSpec((None, n, dh), tok),
            scratch_shapes=[pltpu.VMEM((n, 1), F32), pltpu.VMEM((n, 1), F32), pltpu.VMEM((n, dh), F32),
                            pltpu.VMEM((n, w), F32), pltpu.VMEM((1, w), F32), pltpu.VMEM((pp, w), F32)],
        ),
        out_shape=jax.ShapeDtypeStruct((bd, n, dh), F32),
        compiler_params=_params("parallel", "arbitrary"),
        name="fox_sample",
    )(page_table, q16, k16, v16, lf_row, lf_col, *([cache_k] * pp), *([cache_v] * pp), *([cache_lf] * pp))


def _merge_kernel(x_ref, oa_ref, ob_ref, za_ref, zb_ref, ga_ref, gb_ref, ng_ref, wa_ref, wb_ref, wo_ref,
                  pg_ref, y_ref, *, heads):
    ng = ng_ref[...]
    parts = []
    for h in range(heads):
        cols = slice(h * LANES, (h + 1) * LANES)
        blk = oa_ref[:, cols]
        nrm = blk * lax.rsqrt(jnp.mean(blk * blk, axis=-1, keepdims=True) + RMS_EPS) * ng
        za = za_ref[:, cols]
        parts.append((nrm * (za * _sigmoid(za))).astype(BF16))
    ya = jnp.concatenate(parts, axis=-1)
    zb = zb_ref[...]
    yb = (ob_ref[...] * (zb * _sigmoid(zb))).astype(BF16)
    u = _sigmoid(ga_ref[...]) * _dot(ya, wa_ref[...]) + _sigmoid(gb_ref[...]) * _dot(yb, wb_ref[...])
    mix = _dot(u.astype(BF16), wo_ref[...])
    nrm = mix * lax.rsqrt(jnp.mean(mix * mix, axis=-1, keepdims=True) + RMS_EPS) * pg_ref[...]
    y_ref[...] = x_ref[...] + nrm


def _merge(x, oa, ob, p, norm_g, w_a, w_b, w_out, post_g, *, heads, tm_pref):
    m, d = x.shape
    w = oa.shape[1]
    tm = _tile(m, tm_pref)
    nw = w
    g_col = (8 * nw) // d
    const = lambda i: (0, 0)
    return pl.pallas_call(
        functools.partial(_merge_kernel, heads=heads),
        grid=(m // tm,),
        in_specs=[
            pl.BlockSpec((tm, d), lambda i: (i, 0)),
            pl.BlockSpec((tm, w), lambda i: (i, 0)),
            pl.BlockSpec((tm, w), lambda i: (i, 0)),
            pl.BlockSpec((tm, nw), lambda i: (i, 3)),
            pl.BlockSpec((tm, nw), lambda i: (i, 7)),
            pl.BlockSpec((tm, d), lambda i: (i, g_col)),
            pl.BlockSpec((tm, d), lambda i: (i, g_col + 1)),
            pl.BlockSpec((1, LANES), const),
            pl.BlockSpec(w_a.shape, const),
            pl.BlockSpec(w_b.shape, const),
            pl.BlockSpec(w_out.shape, const),
            pl.BlockSpec((1, d), const),
        ],
        out_specs=pl.BlockSpec((tm, d), lambda i: (i, 0)),
        out_shape=jax.ShapeDtypeStruct((m, d), F32),
        compiler_params=_params("parallel"),
        name="merge_out",
    )(x, oa, ob, p, p, p, p, norm_g, w_a, w_b, w_out, post_g)


def kernel(x_prompt, x_sample, state_hgrn, cache_k, cache_v, cache_logf, page_table, w_in, hgrn_lower_bounds,
           hgrn_norm_g, fox_f_bias, w_branch_a, w_branch_b, w_out, pre_norm_g, post_norm_g):
    bp, seq, d = x_prompt.shape
    bd, tdec, _ = x_sample.shape
    depth, _, heads, dk, dv = state_hgrn.shape
    _, n_pool, page, hb, dh = cache_k.shape
    assert heads == hb and dk == dv == dh == LANES and (8 * heads * dk) % d == 0
    w = heads * dk
    n_pages = page_table.shape[1]
    q_scale = LOG2E * dh ** -0.5

    lb_all = _lower_bounds(hgrn_lower_bounds).reshape(depth, heads, dk)

    w_cat = jnp.concatenate([w_in[:, :, :8 * w], w_in[:, :, 8 * w + heads:]], axis=-1).astype(BF16)
    w_f = jnp.pad(w_in[:, :, 8 * w:8 * w + heads], ((0, 0), (0, 0), (0, LANES - heads))).astype(BF16)
    f_bias = jnp.pad(fox_f_bias.astype(F32), ((0, 0), (0, LANES - heads)))[:, None, :]
    w_a16 = w_branch_a.astype(BF16)
    w_b16 = w_branch_b.astype(BF16)
    w_o16 = w_out.astype(BF16)
    pre_g = pre_norm_g.astype(F32)[:, None, :]
    post_g = post_norm_g.astype(F32)[:, None, :]
    norm_g = hgrn_norm_g.astype(F32)[:, None, :]

    ck = cache_k.reshape(depth, n_pool, page * heads, dh)
    cv = cache_v.reshape(depth, n_pool, page * heads, dh)
    clf = cache_logf.astype(F32).reshape(depth, n_pool, 1, page * heads)

    mp, ms = bp * seq, bd * tdec
    xp = x_prompt.reshape(mp, d)
    xs = x_sample.reshape(ms, d)
    zeros_state = jnp.zeros((bp, heads, dk, dv), F32)
    assert seq % HGRN_CHUNK == 0 and (tdec & (tdec - 1)) == 0
    kp = jnp.zeros((depth, mp, w), F32)
    vp = jnp.zeros((depth, mp, w), F32)
    ks = jnp.zeros((depth, ms, w), F32)
    vs = jnp.zeros((depth, ms, w), F32)
    nh = tdec * heads

    st_p, st_s, lfp_l, lfs_l = [], [], [], []
    for l in range(depth):
        p, lf, qkv16, kp, vp = _in_proj(xp, pre_g[l], w_cat[l], w_f[l], f_bias[l], kp, vp, l,
                                        group_w=w, q_scale=q_scale, tm_pref=TM_PROJ)
        oa, s_fin = _hgrn(p.reshape(bp, seq, -1), lb_all[l], zeros_state, chunk=HGRN_CHUNK, tb_pref=512)
        ccol, crow = _seq_cumsum(lf.reshape(bp, seq, LANES), heads)
        ob = _fox_prompt(qkv16.reshape(bp, seq, 3 * w), ccol, crow, heads=heads, dh=dh)
        xp = _merge(xp, oa.reshape(mp, w), ob.reshape(mp, w), p, norm_g[l], w_a16[l], w_b16[l],
                    w_o16[l], post_g[l], heads=heads, tm_pref=256)
        st_p.append(s_fin)
        lfp_l.append(lf[:, :heads].reshape(bp, seq, heads))

        p, lf, qkv16, ks, vs = _in_proj(xs, pre_g[l], w_cat[l], w_f[l], f_bias[l], ks, vs, l,
                                        group_w=w, q_scale=q_scale, tm_pref=TM_PROJ)
        oa, s_fin = _hgrn(p.reshape(bd, tdec, -1), lb_all[l], state_hgrn[l].astype(F32), chunk=tdec, tb_pref=tdec)
        lf8 = lf[:, :heads].reshape(bd, nh)
        ob = _fox_sample(qkv16[:, :w].reshape(bd, nh, dh), qkv16[:, w:2 * w].reshape(bd, nh, dh),
                         qkv16[:, 2 * w:].reshape(bd, nh, dh), lf8[:, None, :], lf8[:, :, None],
                         ck, cv, clf, page_table, l, heads=heads, pp=_tile(n_pages, PAGES_PER_STEP))
        xs = _merge(xs, oa.reshape(ms, w), ob.reshape(ms, w), p, norm_g[l], w_a16[l], w_b16[l],
                    w_o16[l], post_g[l], heads=heads, tm_pref=256)
        st_s.append(s_fin)
        lfs_l.append(lf[:, :heads].reshape(bd, tdec, heads))

    return (xp.reshape(bp, seq, d), xs.reshape(bd, tdec, d), jnp.stack(st_p), jnp.stack(st_s),
            kp.reshape(depth, bp, seq, heads, dh), vp.reshape(depth, bp, seq, heads, dh), jnp.stack(lfp_l),
            ks.reshape(depth, bd, tdec, heads, dh), vs.reshape(depth, bd, tdec, heads, dh), jnp.stack(lfs_l))
```

```python
import functools

import jax
import jax.numpy as jnp
from jax import lax
from jax.experimental import pallas as pl
from jax.experimental.pallas import tpu as pltpu

F32 = jnp.float32
BF16 = jnp.bfloat16

RMS_EPS = 1e-6
LOG2E = 1.4426950408889634
HGRN_CHUNK = 64
LANES = 128
TM_PROJ = 512
PAGES_PER_STEP = 8
FOX_TILE = 512
FOX_SUB_ROWS = 128
VMEM_LIMIT_BYTES = 60000 * 1024


def _params(*sem):
    return pltpu.CompilerParams(dimension_semantics=sem, vmem_limit_bytes=VMEM_LIMIT_BYTES)


def _tile(n, pref):
    if n <= pref:
        return n
    t = pref
    while n % t:
        t //= 2
    return t


def _dot(a, b):
    return jnp.dot(a, b, preferred_element_type=F32)


def _dot_nt(a, b):
    return lax.dot_general(a, b, (((1,), (1,)), ((), ())), preferred_element_type=F32)


def _dot_tn(a, b):
    return lax.dot_general(a, b, (((0,), (0,)), ((), ())), preferred_element_type=F32)


def _split3(a):
    hi = a.astype(BF16)
    r = a - hi.astype(F32)
    mid = r.astype(BF16)
    lo = (r - mid.astype(F32)).astype(BF16)
    return hi, mid, lo


def _dot01(m01, a):
    hi, mid, lo = _split3(a)
    return _dot(m01, hi) + _dot(m01, mid) + _dot(m01, lo)


def _dot01_right(a, m01):
    hi, mid, lo = _split3(a)
    return _dot(hi, m01) + _dot(mid, m01) + _dot(lo, m01)


def _sigmoid(x):
    return 1.0 / (1.0 + jnp.exp(-x))


def _log_sigmoid(z):
    return jnp.minimum(z, 0.0) - jnp.log(1.0 + jnp.exp(-jnp.abs(z)))


def _lb_kernel(x_ref, o_ref):
    x = x_ref[...]
    depth = x.shape[0]
    e = jnp.exp(x - jnp.max(x, axis=0, keepdims=True))
    sm = e / jnp.sum(e, axis=0, keepdims=True)
    acc = jnp.zeros_like(sm[0:1])
    for i in range(depth):
        o_ref[i:i + 1, :] = acc
        if i + 1 < depth:
            acc = acc + sm[i + 1:i + 2]


def _lower_bounds(lb_param):
    return pl.pallas_call(
        _lb_kernel,
        out_shape=jax.ShapeDtypeStruct(lb_param.shape, F32),
        name="hgrn_lower_bounds",
    )(lb_param.astype(F32))


FA_STEP, KB_STEP, VB_STEP, QB_STEP = 0, 1, 2, 3
MAIN_GROUP_ORDER = (1, 5, 6, 4, 0, 2, 3, 7)
N_MAIN = len(MAIN_GROUP_ORDER)
P16_FIRST_STEP = 4
P16_QA, P16_IA, P16_ZA, P16_ZB, P16_G = 0, 1, 2, 3, 4


def _inproj_kernel(order_ref, x_ref, g_ref, wm_ref, wg_ref, wf_ref, fb_ref, kst_ref, vst_ref,
                   fa_ref, lf_ref, qkv16_ref, p16_ref, k_ref, v_ref, h_scr, *, q_scale):
    del order_ref, kst_ref, vst_ref
    j = pl.program_id(1)

    @pl.when(j == 0)
    def _():
        x = x_ref[...]
        h = x * lax.rsqrt(jnp.mean(x * x, axis=-1, keepdims=True) + RMS_EPS) * g_ref[...]
        hb = h.astype(BF16)
        h_scr[...] = hb
        lf_ref[...] = _log_sigmoid(_dot(hb, wf_ref[...]) + fb_ref[...])

    @pl.when(j == FA_STEP)
    def _():
        fa_ref[...] = _dot(h_scr[...], wm_ref[...])

    @pl.when(j == KB_STEP)
    def _():
        res = _dot(h_scr[...], wm_ref[...])
        k_ref[...] = res
        qkv16_ref[...] = res.astype(BF16)

    @pl.when(j == VB_STEP)
    def _():
        res = _dot(h_scr[...], wm_ref[...])
        v_ref[...] = res
        qkv16_ref[...] = res.astype(BF16)

    @pl.when(j == QB_STEP)
    def _():
        qkv16_ref[...] = (_dot(h_scr[...], wm_ref[...]) * q_scale).astype(BF16)

    @pl.when((j >= P16_FIRST_STEP) & (j < N_MAIN))
    def _():
        p16_ref[...] = _dot(h_scr[...], wm_ref[...]).astype(BF16)

    @pl.when(j >= N_MAIN)
    def _():
        p16_ref[...] = _dot(h_scr[...], wg_ref[...]).astype(BF16)


def _in_proj(x, pre_g, w_main, w_g, w_f, f_bias, k_stack, v_stack, layer, *, group_w, q_scale, tm_pref):
    m, d = x.shape
    tn = group_w
    assert w_main.shape[2] == N_MAIN * tn and w_g.shape[2] % tn == 0
    n_g = w_g.shape[2] // tn
    tm = _tile(m, tm_pref)
    order = jnp.asarray(MAIN_GROUP_ORDER, jnp.int32)
    row = lambda i, j, o: (i, 0)
    return pl.pallas_call(
        functools.partial(_inproj_kernel, q_scale=q_scale),
        grid_spec=pltpu.PrefetchScalarGridSpec(
            num_scalar_prefetch=1,
            grid=(m // tm, N_MAIN + n_g),
            in_specs=[
                pl.BlockSpec((tm, d), row),
                pl.BlockSpec((None, 1, d), lambda i, j, o: (layer, 0, 0)),
                pl.BlockSpec((None, d, tn), lambda i, j, o: (layer, 0, o[jnp.minimum(j, N_MAIN - 1)])),
                pl.BlockSpec((None, d, tn), lambda i, j, o: (layer, 0, jnp.maximum(j - N_MAIN, 0))),
                pl.BlockSpec((None, d, LANES), lambda i, j, o: (layer, 0, 0)),
                pl.BlockSpec((None, 1, LANES), lambda i, j, o: (layer, 0, 0)),
                pl.BlockSpec(memory_space=pl.ANY),
                pl.BlockSpec(memory_space=pl.ANY),
            ],
            out_specs=[
                pl.BlockSpec((tm, tn), row),
                pl.BlockSpec((tm, LANES), row),
                pl.BlockSpec((tm, tn), lambda i, j, o: (i, jnp.clip(j - KB_STEP, 0, 2))),
                pl.BlockSpec((tm, tn), lambda i, j, o: (i, jnp.maximum(j - P16_FIRST_STEP, 0))),
                pl.BlockSpec((None, tm, tn), lambda i, j, o: (layer, i, 0)),
                pl.BlockSpec((None, tm, tn), lambda i, j, o: (layer, i, 0)),
            ],
            scratch_shapes=[pltpu.VMEM((tm, d), BF16)],
        ),
        out_shape=[
            jax.ShapeDtypeStruct((m, tn), F32),
            jax.ShapeDtypeStruct((m, LANES), F32),
            jax.ShapeDtypeStruct((m, 3 * tn), BF16),
            jax.ShapeDtypeStruct((m, (N_MAIN - P16_FIRST_STEP + n_g) * tn), BF16),
            jax.ShapeDtypeStruct(k_stack.shape, F32),
            jax.ShapeDtypeStruct(v_stack.shape, F32),
        ],
        input_output_aliases={7: 4, 8: 5},
        compiler_params=_params("parallel", "arbitrary"),
        name="in_proj",
    )(order, x, pre_g, w_main, w_g, w_f, f_bias, k_stack, v_stack)


def _cumsum_kernel(lf_ref, c_ref, ct_ref, carry):
    @pl.when(pl.program_id(1) == 0)
    def _():
        carry[...] = jnp.zeros_like(carry)

    t = lf_ref.shape[0]
    row = lax.broadcasted_iota(jnp.int32, (t, t), 0)
    col = lax.broadcasted_iota(jnp.int32, (t, t), 1)
    tri = jnp.where(col <= row, 1.0, 0.0).astype(BF16)
    c_nat = _dot01(tri, lf_ref[...]) + carry[...]
    carry[...] = c_nat[t - 1:t, :]
    c = c_nat * LOG2E
    heads = ct_ref.shape[0]
    for h in range(heads):
        c_ref[h] = jnp.broadcast_to(c[:, h:h + 1], c.shape)
    ct_ref[...] = c.T[:heads, :]


def _seq_cumsum(lf, heads):
    b, l, w = lf.shape
    t = _tile(l, 512)
    return pl.pallas_call(
        _cumsum_kernel,
        grid=(b, l // t),
        in_specs=[pl.BlockSpec((None, t, w), lambda i, j: (i, j, 0))],
        out_specs=[pl.BlockSpec((None, heads, t, w), lambda i, j: (i, 0, j, 0)),
                   pl.BlockSpec((None, heads, t), lambda i, j: (i, 0, j))],
        out_shape=[jax.ShapeDtypeStruct((b, heads, l, w), F32), jax.ShapeDtypeStruct((b, heads, l), F32)],
        scratch_shapes=[pltpu.VMEM((1, w), F32)],
        compiler_params=_params("parallel", "arbitrary"),
        name="fox_cumsum",
    )(lf)


def _hgrn_masks(c):
    trow = lax.broadcasted_iota(jnp.int32, (c, LANES), 0)
    ti = lax.broadcasted_iota(jnp.int32, (c, c), 0)
    si = lax.broadcasted_iota(jnp.int32, (c, c), 1)
    tri = jnp.where(si <= ti, 1.0, 0.0).astype(BF16)
    levels = []
    b = 2
    while b <= c:
        half = b // 2
        second = (trow & (b - 1)) >= half
        shift = b.bit_length() - 1
        pair = ((ti >> shift) == (si >> shift)) & ((ti & (b - 1)) >= half) & ((si & (b - 1)) < half)
        levels.append((b, second, pair))
        b *= 2
    return dict(trow=trow, tri=tri, eye=(ti == si), levels=levels)


def _hgrn_ref_rows(g, b, trow):
    c = g.shape[0]
    if b == 2:
        return jnp.where((trow & 1) == 1, pltpu.roll(g, 1, 0), g)
    if b == 4:
        m = trow & 3
        return jnp.where(m == 0, pltpu.roll(g, c - 1, 0),
                         jnp.where(m == 1, g,
                                   jnp.where(m == 2, pltpu.roll(g, 1, 0), pltpu.roll(g, 2, 0))))
    half = b // 2
    g3 = g.reshape(c // b, b, LANES)
    ref = jnp.broadcast_to(g3[:, half - 1:half, :], (c // b, b, LANES))
    return ref.reshape(c, LANES)


def _hgrn_chunk(qs, xs, vs, lbs, sts, masks):
    n = len(qs)
    c = qs[0].shape[0]
    kk, qf, g = [], [], []
    for h in range(n):
        one_m_lb = 1.0 - lbs[h]
        logf = jnp.log(lbs[h] + one_m_lb * _sigmoid(xs[h]))
        kk.append(one_m_lb * _sigmoid(-xs[h]))
        qf.append(qs[h] * _sigmoid(qs[h]))
        g.append(_dot01(masks["tri"], logf))
    vb = [v.astype(BF16) for v in vs]
    o = [_dot_nt((qf[h] * jnp.exp(g[h])).astype(BF16), sts[h].astype(BF16)) for h in range(n)]
    a = [jnp.where(masks["eye"], _dot_nt(qf[h].astype(BF16), kk[h].astype(BF16)), 0.0) for h in range(n)]
    for b, second, pair in masks["levels"]:
        for h in range(n):
            e = jnp.exp(-jnp.abs(g[h] - _hgrn_ref_rows(g[h], b, masks["trow"])))
            qb = jnp.where(second, qf[h] * e, 0.0).astype(BF16)
            kb = jnp.where(second, 0.0, kk[h] * e).astype(BF16)
            a[h] = jnp.where(pair, _dot_nt(qb, kb), a[h])
    o = [o[h] + _dot(a[h].astype(BF16), vb[h]) for h in range(n)]
    st_new = []
    for h in range(n):
        g_end = g[h][c - 1:c, :]
        ke = (kk[h] * jnp.exp(g_end - g[h])).astype(BF16)
        st_new.append(sts[h] * jnp.exp(g_end) + _dot_tn(vb[h], ke))
    return o, st_new


def _hgrn_kernel(q_ref, f_ref, i_ref, lb_ref, s0_ref, o_ref, sfin_ref, st_scr, *, chunk, heads):
    step = pl.program_id(1)
    n_chunks = q_ref.shape[0] // chunk

    @pl.when(step == 0)
    def _():
        for h in range(heads):
            st_scr[h] = s0_ref[h].T

    masks = _hgrn_masks(chunk)

    def body(ci, carry):
        rows = pl.ds(pl.multiple_of(ci * chunk, chunk), chunk)
        cols = [slice(h * LANES, (h + 1) * LANES) for h in range(heads)]
        o, st_new = _hgrn_chunk([q_ref[rows, cl].astype(F32) for cl in cols], [f_ref[rows, cl] for cl in cols],
                                [i_ref[rows, cl].astype(F32) for cl in cols], [lb_ref[h:h + 1, :] for h in range(heads)],
                                [st_scr[h] for h in range(heads)], masks)
        for h in range(heads):
            o_ref[rows, cols[h]] = o[h]
            st_scr[h] = st_new[h]
        return carry

    lax.fori_loop(0, n_chunks, body, 0)

    @pl.when(step == pl.num_programs(1) - 1)
    def _():
        for h in range(heads):
            sfin_ref[h] = st_scr[h].T


def _hgrn(p16, fa, lb, s0, *, chunk, tb_pref):
    bsz, l, _ = p16.shape
    heads, dk = lb.shape
    w = heads * dk
    tb = _tile(l, tb_pref)
    kern = functools.partial(_hgrn_kernel, chunk=chunk, heads=heads)
    return pl.pallas_call(
        kern,
        grid=(bsz, l // tb),
        in_specs=[
            pl.BlockSpec((None, tb, w), lambda b, t: (b, t, P16_QA)),
            pl.BlockSpec((None, tb, w), lambda b, t: (b, t, 0)),
            pl.BlockSpec((None, tb, w), lambda b, t: (b, t, P16_IA)),
            pl.BlockSpec((heads, dk), lambda b, t: (0, 0)),
            pl.BlockSpec((None, heads, dk, dk), lambda b, t: (b, 0, 0, 0)),
        ],
        out_specs=[
            pl.BlockSpec((None, tb, w), lambda b, t: (b, t, 0)),
            pl.BlockSpec((None, heads, dk, dk), lambda b, t: (b, 0, 0, 0)),
        ],
        out_shape=[
            jax.ShapeDtypeStruct((bsz, l, w), F32),
            jax.ShapeDtypeStruct(s0.shape, F32),
        ],
        scratch_shapes=[pltpu.VMEM((heads, dk, dk), F32)],
        compiler_params=_params("parallel", "arbitrary"),
        name="hgrn2",
    )(p16, fa, p16, lb, s0)


def _fox_prompt_kernel(qi_tab, ki_tab, q_ref, k_ref, v_ref, ccol_ref, crow_ref, o_ref, m_scr, l_scr, acc_scr,
                       *, heads, dh):
    step = pl.program_id(1)
    qi = qi_tab[step]
    ki = ki_tab[step]
    tq, tk = q_ref.shape[0], k_ref.shape[0]

    @pl.when(ki == 0)
    def _():
        m_scr[...] = jnp.full_like(m_scr, -jnp.inf)
        l_scr[...] = jnp.zeros_like(l_scr)
        acc_scr[...] = jnp.zeros_like(acc_scr)

    def block(diagonal):
        sub = min(tq, FOX_SUB_ROWS)
        units = [(h, r0) for h in range(heads) for r0 in range(0, tq, sub)]

        def scores(h, r0):
            kw = r0 + sub if diagonal else tk
            return _dot_nt(q_ref[r0:r0 + sub, h * dh:(h + 1) * dh], k_ref[:kw, h * dh:(h + 1) * dh])

        s_next = scores(*units[0])
        for u, (h, r0) in enumerate(units):
            s = s_next
            if u + 1 < len(units):
                s_next = scores(*units[u + 1])
            rows = slice(r0, r0 + sub)
            cols = slice(h * dh, (h + 1) * dh)
            kw = s.shape[1]
            s = s - crow_ref[h:h + 1, :kw]
            if diagonal:
                row = lax.broadcasted_iota(jnp.int32, (sub, kw), 0) + r0
                col = lax.broadcasted_iota(jnp.int32, (sub, kw), 1)
                s = jnp.where(col <= row, s, -jnp.inf)
            cc = ccol_ref[h, rows, :]
            m_old = m_scr[h, rows, :]
            m_new = jnp.maximum(m_old, jnp.max(s, axis=-1, keepdims=True) + cc)
            p = jnp.exp2(s - jnp.concatenate([m_new - cc] * (kw // LANES), axis=1))
            alpha = jnp.exp2(m_old - m_new)
            l_new = alpha * l_scr[h, rows, :] + jnp.sum(p, axis=-1, keepdims=True)
            acc = alpha * acc_scr[rows, cols] + _dot(p.astype(BF16), v_ref[:kw, cols])
            m_scr[h, rows, :] = m_new
            if diagonal:
                o_ref[rows, cols] = acc / l_new
            else:
                l_scr[h, rows, :] = l_new
                acc_scr[rows, cols] = acc

    @pl.when(ki < qi)
    def _():
        block(False)

    @pl.when(ki == qi)
    def _():
        block(True)


def _fox_prompt(qkv16, ccol, crow, *, heads, dh):
    assert dh == LANES
    bsz, l, _ = qkv16.shape
    w = heads * dh
    t = _tile(l, FOX_TILE)
    nq = l // t
    pairs = [(qi, ki) for qi in range(nq) for ki in range(qi + 1)]
    qi_tab = jnp.asarray([p[0] for p in pairs], jnp.int32)
    ki_tab = jnp.asarray([p[1] for p in pairs], jnp.int32)
    return pl.pallas_call(
        functools.partial(_fox_prompt_kernel, heads=heads, dh=dh),
        grid_spec=pltpu.PrefetchScalarGridSpec(
            num_scalar_prefetch=2,
            grid=(bsz, len(pairs)),
            in_specs=[
                pl.BlockSpec((None, t, w), lambda b, s, qt, kt: (b, qt[s], 2)),
                pl.BlockSpec((None, t, w), lambda b, s, qt, kt: (b, kt[s], 0)),
                pl.BlockSpec((None, t, w), lambda b, s, qt, kt: (b, kt[s], 1)),
                pl.BlockSpec((None, heads, t, LANES), lambda b, s, qt, kt: (b, 0, qt[s], 0)),
                pl.BlockSpec((None, heads, t), lambda b, s, qt, kt: (b, 0, kt[s])),
            ],
            out_specs=pl.BlockSpec((None, t, w), lambda b, s, qt, kt: (b, qt[s], 0)),
            scratch_shapes=[pltpu.VMEM((heads, t, LANES), F32), pltpu.VMEM((heads, t, LANES), F32),
                            pltpu.VMEM((t, w), F32)],
        ),
        out_shape=jax.ShapeDtypeStruct((bsz, l, w), F32),
        compiler_params=_params("parallel", "arbitrary"),
        name="fox_prompt",
    )(qi_tab, ki_tab, qkv16, qkv16, qkv16, ccol, crow)


def _fox_sample_kernel(pt_ref, q_ref, kn_ref, vn_ref, lfr_ref, lfc_ref, *refs, heads, pp):
    del pt_ref
    k_refs, v_refs, lf_refs = refs[:pp], refs[pp:2 * pp], refs[2 * pp:3 * pp]
    o_ref = refs[3 * pp]
    m_scr, l_scr, acc_scr, rb_scr, carry_scr, lf_scr = refs[3 * pp + 1:]
    g = pl.program_id(1)
    n = q_ref.shape[0]
    w = k_refs[0].shape[0]
    qb = q_ref[...]

    @pl.when(g == 0)
    def _():
        ii = lax.broadcasted_iota(jnp.int32, (n, n), 0)
        jj = lax.broadcasted_iota(jnp.int32, (n, n), 1)
        same_head = (ii & (heads - 1)) == (jj & (heads - 1))
        m_row = jnp.where(same_head & (ii <= jj), 1.0, 0.0).astype(BF16)
        m_col = jnp.where(same_head & (jj <= ii), 1.0, 0.0).astype(BF16)
        cn_row = _dot01_right(jnp.broadcast_to(lfr_ref[...], (8, n)), m_row)[0:1, :] * LOG2E
        cn_col = _dot01(m_col, lfc_ref[...]) * LOG2E
        iw = lax.broadcasted_iota(jnp.int32, (n, w), 0)
        jw = lax.broadcasted_iota(jnp.int32, (n, w), 1)
        rb_scr[...] = jnp.where((iw & (heads - 1)) == (jw & (heads - 1)), cn_col, -jnp.inf)
        carry_scr[...] = jnp.zeros_like(carry_scr)
        s = _dot_nt(qb, kn_ref[...]) + cn_col - cn_row
        s = jnp.where(same_head & (jj <= ii), s, -jnp.inf)
        m0 = jnp.max(s, axis=-1, keepdims=True)
        p = jnp.exp2(s - m0)
        m_scr[...] = m0
        l_scr[...] = jnp.sum(p, axis=-1, keepdims=True)
        acc_scr[...] = _dot(p.astype(BF16), vn_ref[...])

    for j in range(pp):
        lf_scr[j:j + 1, :] = lf_refs[j][...]
    x = lf_scr[...] * LOG2E
    lane = lax.broadcasted_iota(jnp.int32, (pp, w), 1)
    incl, tot = x, x
    d = heads
    while d < w:
        incl = incl + jnp.where(lane < w - d, pltpu.roll(incl, w - d, 1), 0.0)
        tot = tot + pltpu.roll(tot, d, 1)
        d *= 2
    excl = jnp.where(lane < w - heads, pltpu.roll(incl, w - heads, 1), 0.0)
    run = carry_scr[...]
    bias = [None] * pp
    for j in reversed(range(pp)):
        bias[j] = excl[j:j + 1, :] + run
        run = run + tot[j:j + 1, :]
    carry_scr[...] = run

    rb = rb_scr[...]
    s_list = [_dot_nt(qb, k_refs[j][...].astype(BF16)) + rb + bias[j] for j in range(pp)]
    mx = s_list[0]
    for s in s_list[1:]:
        mx = jnp.maximum(mx, s)
    m_old = m_scr[...]
    m_new = jnp.maximum(m_old, jnp.max(mx, axis=-1, keepdims=True))
    alpha = jnp.exp2(m_old - m_new)
    l_new = alpha * l_scr[...]
    acc = alpha * acc_scr[...]
    for j in range(pp):
        p = jnp.exp2(s_list[j] - m_new)
        l_new = l_new + jnp.sum(p, axis=-1, keepdims=True)
        acc = acc + _dot(p.astype(BF16), v_refs[j][...].astype(BF16))
    m_scr[...] = m_new
    l_scr[...] = l_new
    acc_scr[...] = acc

    @pl.when(g == pl.num_programs(1) - 1)
    def _():
        o_ref[...] = acc / l_new


def _fox_sample(q16, k16, v16, lf_row, lf_col, cache_k, cache_v, cache_lf, page_table, layer, *, heads, pp):
    bd, n, dh = q16.shape
    w = cache_k.shape[2]
    n_pages = page_table.shape[1]
    assert n_pages % pp == 0
    steps = n_pages // pp
    tok = lambda b, g, pt: (b, 0, 0)

    def page(j):
        return lambda b, g, pt: (layer, pt[b, n_pages - pp * (g + 1) + j], 0, 0)

    return pl.pallas_call(
        functools.partial(_fox_sample_kernel, heads=heads, pp=pp),
        grid_spec=pltpu.PrefetchScalarGridSpec(
            num_scalar_prefetch=1,
            grid=(bd, steps),
            in_specs=[
                pl.BlockSpec((None, n, dh), tok),
                pl.BlockSpec((None, n, dh), tok),
                pl.BlockSpec((None, n, dh), tok),
                pl.BlockSpec((None, 1, n), tok),
                pl.BlockSpec((None, n, 1), tok),
            ] + [pl.BlockSpec((None, None, w, dh), page(j)) for j in range(pp)]
              + [pl.BlockSpec((None, None, w, dh), page(j)) for j in range(pp)]
              + [pl.BlockSpec((None, None, 1, w), page(j)) for j in range(pp)],
            out_specs=pl.BlockSpec((None, n, dh), tok),
            scratch_shapes=[pltpu.VMEM((n, 1), F32), pltpu.VMEM((n, 1), F32), pltpu.VMEM((n, dh), F32),
                            pltpu.VMEM((n, w), F32), pltpu.VMEM((1, w), F32), pltpu.VMEM((pp, w), F32)],
        ),
        out_shape=jax.ShapeDtypeStruct((bd, n, dh), F32),
        compiler_params=_params("parallel", "arbitrary"),
        name="fox_sample",
    )(page_table, q16, k16, v16, lf_row, lf_col, *([cache_k] * pp), *([cache_v] * pp), *([cache_lf] * pp))


def _merge_kernel(x_ref, oa_ref, ob_ref, za_ref, zb_ref, ga_ref, gb_ref, ng_ref, wa_ref, wb_ref, wo_ref,
                  pg_ref, y_ref, *, heads):
    ng = ng_ref[...]
    parts = []
    for h in range(heads):
        cols = slice(h * LANES, (h + 1) * LANES)
        blk = oa_ref[:, cols]
        nrm = blk * lax.rsqrt(jnp.mean(blk * blk, axis=-1, keepdims=True) + RMS_EPS) * ng
        za = za_ref[:, cols].astype(F32)
        parts.append((nrm * (za * _sigmoid(za))).astype(BF16))
    ya = jnp.concatenate(parts, axis=-1)
    zb = zb_ref[...].astype(F32)
    yb = (ob_ref[...] * (zb * _sigmoid(zb))).astype(BF16)
    u = (_sigmoid(ga_ref[...].astype(F32)) * _dot(ya, wa_ref[...])
         + _sigmoid(gb_ref[...].astype(F32)) * _dot(yb, wb_ref[...]))
    mix = _dot(u.astype(BF16), wo_ref[...])
    nrm = mix * lax.rsqrt(jnp.mean(mix * mix, axis=-1, keepdims=True) + RMS_EPS) * pg_ref[...]
    y_ref[...] = x_ref[...] + nrm


def _merge(x, oa, ob, p16, norm_g, w_a, w_b, w_out, post_g, layer, *, heads, tm_pref):
    m, d = x.shape
    w = oa.shape[1]
    tm = _tile(m, tm_pref)
    assert (P16_G * w) % d == 0
    g_col = (P16_G * w) // d
    row = lambda i: (i, 0)
    lay = lambda i: (layer, 0, 0)
    return pl.pallas_call(
        functools.partial(_merge_kernel, heads=heads),
        grid=(m // tm,),
        in_specs=[
            pl.BlockSpec((tm, d), row),
            pl.BlockSpec((tm, w), row),
            pl.BlockSpec((tm, w), row),
            pl.BlockSpec((tm, w), lambda i: (i, P16_ZA)),
            pl.BlockSpec((tm, w), lambda i: (i, P16_ZB)),
            pl.BlockSpec((tm, d), lambda i: (i, g_col)),
            pl.BlockSpec((tm, d), lambda i: (i, g_col + 1)),
            pl.BlockSpec((None, 1, LANES), lay),
            pl.BlockSpec((None,) + w_a.shape[1:], lay),
            pl.BlockSpec((None,) + w_b.shape[1:], lay),
            pl.BlockSpec((None,) + w_out.shape[1:], lay),
            pl.BlockSpec((None, 1, d), lay),
        ],
        out_specs=pl.BlockSpec((tm, d), row),
        out_shape=jax.ShapeDtypeStruct((m, d), F32),
        compiler_params=_params("parallel"),
        name="merge_out",
    )(x, oa, ob, p16, p16, p16, p16, norm_g, w_a, w_b, w_out, post_g)


def kernel(x_prompt, x_sample, state_hgrn, cache_k, cache_v, cache_logf, page_table, w_in, hgrn_lower_bounds,
           hgrn_norm_g, fox_f_bias, w_branch_a, w_branch_b, w_out, pre_norm_g, post_norm_g):
    bp, seq, d = x_prompt.shape
    bd, tdec, _ = x_sample.shape
    depth, _, heads, dk, dv = state_hgrn.shape
    _, n_pool, page, hb, dh = cache_k.shape
    assert heads == hb and dk == dv == dh == LANES
    w = heads * dk
    n_pages = page_table.shape[1]
    q_scale = LOG2E * dh ** -0.5

    lb_all = _lower_bounds(hgrn_lower_bounds).reshape(depth, heads, dk)

    w_main = w_in[:, :, :N_MAIN * w].astype(BF16)
    w_g = w_in[:, :, N_MAIN * w + heads:].astype(BF16)
    w_f = jnp.pad(w_in[:, :, N_MAIN * w:N_MAIN * w + heads], ((0, 0), (0, 0), (0, LANES - heads))).astype(BF16)
    f_bias = jnp.pad(fox_f_bias.astype(F32), ((0, 0), (0, LANES - heads)))[:, None, :]
    w_a16 = w_branch_a.astype(BF16)
    w_b16 = w_branch_b.astype(BF16)
    w_o16 = w_out.astype(BF16)
    pre_g = pre_norm_g.astype(F32)[:, None, :]
    post_g = post_norm_g.astype(F32)[:, None, :]
    norm_g = hgrn_norm_g.astype(F32)[:, None, :]

    ck = cache_k.reshape(depth, n_pool, page * heads, dh)
    cv = cache_v.reshape(depth, n_pool, page * heads, dh)
    clf = cache_logf.astype(F32).reshape(depth, n_pool, 1, page * heads)

    mp, ms = bp * seq, bd * tdec
    xp = x_prompt.reshape(mp, d)
    xs = x_sample.reshape(ms, d)
    zeros_state = jnp.zeros((bp, heads, dk, dv), F32)
    assert seq % HGRN_CHUNK == 0 and (tdec & (tdec - 1)) == 0
    kp = jnp.zeros((depth, mp, w), F32)
    vp = jnp.zeros((depth, mp, w), F32)
    ks = jnp.zeros((depth, ms, w), F32)
    vs = jnp.zeros((depth, ms, w), F32)
    nh = tdec * heads

    st_p, st_s, lfp_l, lfs_l = [], [], [], []
    for l in range(depth):
        fa, lf, qkv16, p16, kp, vp = _in_proj(xp, pre_g, w_main, w_g, w_f, f_bias, kp, vp, l,
                                              group_w=w, q_scale=q_scale, tm_pref=TM_PROJ)
        oa, s_fin = _hgrn(p16.reshape(bp, seq, -1), fa.reshape(bp, seq, w), lb_all[l], zeros_state,
                          chunk=HGRN_CHUNK, tb_pref=512)
        ccol, crow = _seq_cumsum(lf.reshape(bp, seq, LANES), heads)
        ob = _fox_prompt(qkv16.reshape(bp, seq, 3 * w), ccol, crow, heads=heads, dh=dh)
        xp = _merge(xp, oa.reshape(mp, w), ob.reshape(mp, w), p16, norm_g, w_a16, w_b16, w_o16, post_g, l,
                    heads=heads, tm_pref=256)
        st_p.append(s_fin)
        lfp_l.append(lf[:, :heads].reshape(bp, seq, heads))

        fa, lf, qkv16, p16, ks, vs = _in_proj(xs, pre_g, w_main, w_g, w_f, f_bias, ks, vs, l,
                                              group_w=w, q_scale=q_scale, tm_pref=TM_PROJ)
        oa, s_fin = _hgrn(p16.reshape(bd, tdec, -1), fa.reshape(bd, tdec, w), lb_all[l],
                          state_hgrn[l].astype(F32), chunk=tdec, tb_pref=tdec)
        lf8 = lf[:, :heads].reshape(bd, nh)
        ob = _fox_sample(qkv16[:, 2 * w:].reshape(bd, nh, dh), qkv16[:, :w].reshape(bd, nh, dh),
                         qkv16[:, w:2 * w].reshape(bd, nh, dh), lf8[:, None, :], lf8[:, :, None],
                         ck, cv, clf, page_table, l, heads=heads, pp=_tile(n_pages, PAGES_PER_STEP))
        xs = _merge(xs, oa.reshape(ms, w), ob.reshape(ms, w), p16, norm_g, w_a16, w_b16, w_o16, post_g, l,
                    heads=heads, tm_pref=256)
        st_s.append(s_fin)
        lfs_l.append(lf[:, :heads].reshape(bd, tdec, heads))

    return (xp.reshape(bp, seq, d), xs.reshape(bd, tdec, d), jnp.stack(st_p), jnp.stack(st_s),
            kp.reshape(depth, bp, seq, heads, dh), vp.reshape(depth, bp, seq, heads, dh), jnp.stack(lfp_l),
            ks.reshape(depth, bd, tdec, heads, dh), vs.reshape(depth, bd, tdec, heads, dh), jnp.stack(lfs_l))
```

```python
import functools

import jax
import jax.numpy as jnp
from jax import lax
from jax.experimental import pallas as pl
from jax.experimental.pallas import tpu as pltpu

F32 = jnp.float32
BF16 = jnp.bfloat16

RMS_EPS = 1e-6
LOG2E = 1.4426950408889634
HGRN_CHUNK = 64
LANES = 128
TM_PROJ = 512
PAGES_PER_STEP = 8
FOX_TILE = 512
FOX_SUB_ROWS = 256
VMEM_LIMIT_BYTES = 60000 * 1024


def _params(*sem):
    return pltpu.CompilerParams(dimension_semantics=sem, vmem_limit_bytes=VMEM_LIMIT_BYTES)


def _tile(n, pref):
    if n <= pref:
        return n
    t = pref
    while n % t:
        t //= 2
    return t


def _dot(a, b):
    return jnp.dot(a, b, preferred_element_type=F32)


def _dot_nt(a, b):
    return lax.dot_general(a, b, (((1,), (1,)), ((), ())), preferred_element_type=F32)


def _dot_tn(a, b):
    return lax.dot_general(a, b, (((0,), (0,)), ((), ())), preferred_element_type=F32)


def _split3(a):
    hi = a.astype(BF16)
    r = a - hi.astype(F32)
    mid = r.astype(BF16)
    lo = (r - mid.astype(F32)).astype(BF16)
    return hi, mid, lo


def _dot01(m01, a):
    hi, mid, lo = _split3(a)
    return _dot(m01, hi) + _dot(m01, mid) + _dot(m01, lo)


def _dot01_right(a, m01):
    hi, mid, lo = _split3(a)
    return _dot(hi, m01) + _dot(mid, m01) + _dot(lo, m01)


def _sigmoid(x):
    return 1.0 / (1.0 + jnp.exp(-x))


def _log_sigmoid(z):
    return jnp.minimum(z, 0.0) - jnp.log(1.0 + jnp.exp(-jnp.abs(z)))


def _lb_kernel(x_ref, o_ref):
    x = x_ref[...]
    depth = x.shape[0]
    e = jnp.exp(x - jnp.max(x, axis=0, keepdims=True))
    sm = e / jnp.sum(e, axis=0, keepdims=True)
    acc = jnp.zeros_like(sm[0:1])
    for i in range(depth):
        o_ref[i:i + 1, :] = acc
        if i + 1 < depth:
            acc = acc + sm[i + 1:i + 2]


def _lower_bounds(lb_param):
    return pl.pallas_call(
        _lb_kernel,
        out_shape=jax.ShapeDtypeStruct(lb_param.shape, F32),
        name="hgrn_lower_bounds",
    )(lb_param.astype(F32))


FA_STEP, KB_STEP, VB_STEP, QB_STEP = 0, 1, 2, 3
MAIN_GROUP_ORDER = (1, 5, 6, 4, 0, 2, 3, 7)
N_MAIN = len(MAIN_GROUP_ORDER)
P16_FIRST_STEP = 4
P16_QA, P16_IA, P16_ZA, P16_ZB, P16_G = 0, 1, 2, 3, 4


def _inproj_kernel(order_ref, x_ref, g_ref, wm_ref, wg_ref, wf_ref, fb_ref, kst_ref, vst_ref,
                   fa_ref, lf_ref, qkv16_ref, p16_ref, k_ref, v_ref, h_scr, *, q_scale):
    del order_ref, kst_ref, vst_ref
    j = pl.program_id(1)

    @pl.when(j == 0)
    def _():
        x = x_ref[...]
        h = x * lax.rsqrt(jnp.mean(x * x, axis=-1, keepdims=True) + RMS_EPS) * g_ref[...]
        hb = h.astype(BF16)
        h_scr[...] = hb
        lf_ref[...] = _log_sigmoid(_dot_nt(hb, wf_ref[...]) + fb_ref[...])

    @pl.when(j == FA_STEP)
    def _():
        fa_ref[...] = _dot_nt(h_scr[...], wm_ref[...])

    @pl.when(j == KB_STEP)
    def _():
        res = _dot_nt(h_scr[...], wm_ref[...])
        k_ref[...] = res
        qkv16_ref[...] = res.astype(BF16)

    @pl.when(j == VB_STEP)
    def _():
        res = _dot_nt(h_scr[...], wm_ref[...])
        v_ref[...] = res
        qkv16_ref[...] = res.astype(BF16)

    @pl.when(j == QB_STEP)
    def _():
        qkv16_ref[...] = (_dot_nt(h_scr[...], wm_ref[...]) * q_scale).astype(BF16)

    @pl.when((j >= P16_FIRST_STEP) & (j < N_MAIN))
    def _():
        p16_ref[...] = _dot_nt(h_scr[...], wm_ref[...]).astype(BF16)

    @pl.when(j >= N_MAIN)
    def _():
        p16_ref[...] = _dot_nt(h_scr[...], wg_ref[...]).astype(BF16)


def _in_proj(x, pre_g, w_main, w_g, w_f, f_bias, k_stack, v_stack, layer, *, group_w, q_scale, tm_pref):
    m, d = x.shape
    tn = group_w
    assert w_main.shape[1] == N_MAIN * tn and w_g.shape[1] % tn == 0
    n_g = w_g.shape[1] // tn
    tm = _tile(m, tm_pref)
    order = jnp.asarray(MAIN_GROUP_ORDER, jnp.int32)
    row = lambda i, j, o: (i, 0)
    return pl.pallas_call(
        functools.partial(_inproj_kernel, q_scale=q_scale),
        grid_spec=pltpu.PrefetchScalarGridSpec(
            num_scalar_prefetch=1,
            grid=(m // tm, N_MAIN + n_g),
            in_specs=[
                pl.BlockSpec((tm, d), row),
                pl.BlockSpec((None, 1, d), lambda i, j, o: (layer, 0, 0)),
                pl.BlockSpec((None, tn, d), lambda i, j, o: (layer, o[jnp.minimum(j, N_MAIN - 1)], 0)),
                pl.BlockSpec((None, tn, d), lambda i, j, o: (layer, jnp.maximum(j - N_MAIN, 0), 0)),
                pl.BlockSpec((None, LANES, d), lambda i, j, o: (layer, 0, 0)),
                pl.BlockSpec((None, 1, LANES), lambda i, j, o: (layer, 0, 0)),
                pl.BlockSpec(memory_space=pl.ANY),
                pl.BlockSpec(memory_space=pl.ANY),
            ],
            out_specs=[
                pl.BlockSpec((tm, tn), row),
                pl.BlockSpec((tm, LANES), row),
                pl.BlockSpec((tm, tn), lambda i, j, o: (i, jnp.clip(j - KB_STEP, 0, 2))),
                pl.BlockSpec((tm, tn), lambda i, j, o: (i, jnp.maximum(j - P16_FIRST_STEP, 0))),
                pl.BlockSpec((None, tm, tn), lambda i, j, o: (layer, i, 0)),
                pl.BlockSpec((None, tm, tn), lambda i, j, o: (layer, i, 0)),
            ],
            scratch_shapes=[pltpu.VMEM((tm, d), BF16)],
        ),
        out_shape=[
            jax.ShapeDtypeStruct((m, tn), F32),
            jax.ShapeDtypeStruct((m, LANES), F32),
            jax.ShapeDtypeStruct((m, 3 * tn), BF16),
            jax.ShapeDtypeStruct((m, (N_MAIN - P16_FIRST_STEP + n_g) * tn), BF16),
            jax.ShapeDtypeStruct(k_stack.shape, F32),
            jax.ShapeDtypeStruct(v_stack.shape, F32),
        ],
        input_output_aliases={7: 4, 8: 5},
        compiler_params=_params("parallel", "arbitrary"),
        name="in_proj",
    )(order, x, pre_g, w_main, w_g, w_f, f_bias, k_stack, v_stack)


def _cumsum_kernel(lf_ref, c_ref, ct_ref, carry):
    @pl.when(pl.program_id(1) == 0)
    def _():
        carry[...] = jnp.zeros_like(carry)

    t = lf_ref.shape[0]
    row = lax.broadcasted_iota(jnp.int32, (t, t), 0)
    col = lax.broadcasted_iota(jnp.int32, (t, t), 1)
    tri = jnp.where(col <= row, 1.0, 0.0).astype(BF16)
    c_nat = _dot01(tri, lf_ref[...]) + carry[...]
    carry[...] = c_nat[t - 1:t, :]
    c = c_nat * LOG2E
    heads = ct_ref.shape[0]
    for h in range(heads):
        c_ref[h] = jnp.broadcast_to(c[:, h:h + 1], c.shape)
    ct_ref[...] = c.T[:heads, :]


def _seq_cumsum(lf, heads):
    b, l, w = lf.shape
    t = _tile(l, 512)
    return pl.pallas_call(
        _cumsum_kernel,
        grid=(b, l // t),
        in_specs=[pl.BlockSpec((None, t, w), lambda i, j: (i, j, 0))],
        out_specs=[pl.BlockSpec((None, heads, t, w), lambda i, j: (i, 0, j, 0)),
                   pl.BlockSpec((None, heads, t), lambda i, j: (i, 0, j))],
        out_shape=[jax.ShapeDtypeStruct((b, heads, l, w), F32), jax.ShapeDtypeStruct((b, heads, l), F32)],
        scratch_shapes=[pltpu.VMEM((1, w), F32)],
        compiler_params=_params("parallel", "arbitrary"),
        name="fox_cumsum",
    )(lf)


def _hgrn_masks(c):
    trow = lax.broadcasted_iota(jnp.int32, (c, LANES), 0)
    ti = lax.broadcasted_iota(jnp.int32, (c, c), 0)
    si = lax.broadcasted_iota(jnp.int32, (c, c), 1)
    tri = jnp.where(si <= ti, 1.0, 0.0).astype(BF16)
    levels = []
    b = 2
    while b <= c:
        half = b // 2
        second = (trow & (b - 1)) >= half
        shift = b.bit_length() - 1
        pair = ((ti >> shift) == (si >> shift)) & ((ti & (b - 1)) >= half) & ((si & (b - 1)) < half)
        levels.append((b, second, pair))
        b *= 2
    return dict(trow=trow, tri=tri, eye=(ti == si), levels=levels)


def _hgrn_ref_rows(g, b, trow):
    c = g.shape[0]
    if b == 2:
        return jnp.where((trow & 1) == 1, pltpu.roll(g, 1, 0), g)
    if b == 4:
        m = trow & 3
        return jnp.where(m == 0, pltpu.roll(g, c - 1, 0),
                         jnp.where(m == 1, g,
                                   jnp.where(m == 2, pltpu.roll(g, 1, 0), pltpu.roll(g, 2, 0))))
    half = b // 2
    g3 = g.reshape(c // b, b, LANES)
    ref = jnp.broadcast_to(g3[:, half - 1:half, :], (c // b, b, LANES))
    return ref.reshape(c, LANES)


def _hgrn_chunk(qs, xs, vs, lbs, sts, masks):
    n = len(qs)
    c = qs[0].shape[0]
    kk, qf, g = [], [], []
    for h in range(n):
        one_m_lb = 1.0 - lbs[h]
        logf = jnp.log(lbs[h] + one_m_lb * _sigmoid(xs[h]))
        kk.append(one_m_lb * _sigmoid(-xs[h]))
        qf.append(qs[h] * _sigmoid(qs[h]))
        g.append(_dot01(masks["tri"], logf))
    vb = [v.astype(BF16) for v in vs]
    o = [_dot_nt((qf[h] * jnp.exp(g[h])).astype(BF16), sts[h].astype(BF16)) for h in range(n)]
    a = [jnp.where(masks["eye"], _dot_nt(qf[h].astype(BF16), kk[h].astype(BF16)), 0.0) for h in range(n)]
    for b, second, pair in masks["levels"]:
        for h in range(n):
            e = jnp.exp(-jnp.abs(g[h] - _hgrn_ref_rows(g[h], b, masks["trow"])))
            qb = jnp.where(second, qf[h] * e, 0.0).astype(BF16)
            kb = jnp.where(second, 0.0, kk[h] * e).astype(BF16)
            a[h] = jnp.where(pair, _dot_nt(qb, kb), a[h])
    o = [o[h] + _dot(a[h].astype(BF16), vb[h]) for h in range(n)]
    st_new = []
    for h in range(n):
        g_end = g[h][c - 1:c, :]
        ke = (kk[h] * jnp.exp(g_end - g[h])).astype(BF16)
        st_new.append(sts[h] * jnp.exp(g_end) + _dot_tn(vb[h], ke))
    return o, st_new


def _hgrn_kernel(q_ref, f_ref, i_ref, lb_ref, s0_ref, o_ref, sfin_ref, st_scr, *, chunk, heads):
    step = pl.program_id(1)
    n_chunks = q_ref.shape[0] // chunk

    @pl.when(step == 0)
    def _():
        for h in range(heads):
            st_scr[h] = s0_ref[h].T

    masks = _hgrn_masks(chunk)

    def body(ci, carry):
        rows = pl.ds(pl.multiple_of(ci * chunk, chunk), chunk)
        cols = [slice(h * LANES, (h + 1) * LANES) for h in range(heads)]
        o, st_new = _hgrn_chunk([q_ref[rows, cl].astype(F32) for cl in cols], [f_ref[rows, cl] for cl in cols],
                                [i_ref[rows, cl].astype(F32) for cl in cols], [lb_ref[h:h + 1, :] for h in range(heads)],
                                [st_scr[h] for h in range(heads)], masks)
        for h in range(heads):
            o_ref[rows, cols[h]] = o[h]
            st_scr[h] = st_new[h]
        return carry

    lax.fori_loop(0, n_chunks, body, 0)

    @pl.when(step == pl.num_programs(1) - 1)
    def _():
        for h in range(heads):
            sfin_ref[h] = st_scr[h].T


def _hgrn(p16, fa, lb, s0, *, chunk, tb_pref):
    bsz, l, _ = p16.shape
    heads, dk = lb.shape
    w = heads * dk
    tb = _tile(l, tb_pref)
    kern = functools.partial(_hgrn_kernel, chunk=chunk, heads=heads)
    return pl.pallas_call(
        kern,
        grid=(bsz, l // tb),
        in_specs=[
            pl.BlockSpec((None, tb, w), lambda b, t: (b, t, P16_QA)),
            pl.BlockSpec((None, tb, w), lambda b, t: (b, t, 0)),
            pl.BlockSpec((None, tb, w), lambda b, t: (b, t, P16_IA)),
            pl.BlockSpec((heads, dk), lambda b, t: (0, 0)),
            pl.BlockSpec((None, heads, dk, dk), lambda b, t: (b, 0, 0, 0)),
        ],
        out_specs=[
            pl.BlockSpec((None, tb, w), lambda b, t: (b, t, 0)),
            pl.BlockSpec((None, heads, dk, dk), lambda b, t: (b, 0, 0, 0)),
        ],
        out_shape=[
            jax.ShapeDtypeStruct((bsz, l, w), F32),
            jax.ShapeDtypeStruct(s0.shape, F32),
        ],
        scratch_shapes=[pltpu.VMEM((heads, dk, dk), F32)],
        compiler_params=_params("parallel", "arbitrary"),
        name="hgrn2",
    )(p16, fa, p16, lb, s0)


def _fox_prompt_kernel(qi_tab, ki_tab, q_ref, k_ref, v_ref, ccol_ref, crow_ref, o_ref, m_scr, l_scr, acc_scr,
                       *, heads, dh):
    step = pl.program_id(1)
    qi = qi_tab[step]
    ki = ki_tab[step]
    tq, tk = q_ref.shape[0], k_ref.shape[0]

    @pl.when(ki == 0)
    def _():
        m_scr[...] = jnp.full_like(m_scr, -jnp.inf)
        l_scr[...] = jnp.zeros_like(l_scr)
        acc_scr[...] = jnp.zeros_like(acc_scr)

    def block(diagonal):
        sub = min(tq, FOX_SUB_ROWS)
        units = [(h, r0) for h in range(heads) for r0 in range(0, tq, sub)]

        def scores(h, r0):
            kw = r0 + sub if diagonal else tk
            return _dot_nt(q_ref[r0:r0 + sub, h * dh:(h + 1) * dh], k_ref[:kw, h * dh:(h + 1) * dh])

        s_next = scores(*units[0])
        for u, (h, r0) in enumerate(units):
            s = s_next
            if u + 1 < len(units):
                s_next = scores(*units[u + 1])
            rows = slice(r0, r0 + sub)
            cols = slice(h * dh, (h + 1) * dh)
            kw = s.shape[1]
            s = s - crow_ref[h:h + 1, :kw]
            if diagonal:
                row = lax.broadcasted_iota(jnp.int32, (sub, kw), 0) + r0
                col = lax.broadcasted_iota(jnp.int32, (sub, kw), 1)
                s = jnp.where(col <= row, s, -jnp.inf)
            cc = ccol_ref[h, rows, :]
            m_old = m_scr[h, rows, :]
            m_new = jnp.maximum(m_old, jnp.max(s, axis=-1, keepdims=True) + cc)
            p = jnp.exp2(s - jnp.concatenate([m_new - cc] * (kw // LANES), axis=1))
            alpha = jnp.exp2(m_old - m_new)
            l_new = alpha * l_scr[h, rows, :] + jnp.sum(p, axis=-1, keepdims=True)
            acc = alpha * acc_scr[rows, cols] + _dot(p.astype(BF16), v_ref[:kw, cols])
            m_scr[h, rows, :] = m_new
            if diagonal:
                o_ref[rows, cols] = acc / l_new
            else:
                l_scr[h, rows, :] = l_new
                acc_scr[rows, cols] = acc

    @pl.when(ki < qi)
    def _():
        block(False)

    @pl.when(ki == qi)
    def _():
        block(True)


def _fox_prompt(qkv16, ccol, crow, *, heads, dh):
    assert dh == LANES
    bsz, l, _ = qkv16.shape
    w = heads * dh
    t = _tile(l, FOX_TILE)
    nq = l // t
    pairs = [(qi, ki) for qi in range(nq) for ki in range(qi + 1)]
    qi_tab = jnp.asarray([p[0] for p in pairs], jnp.int32)
    ki_tab = jnp.asarray([p[1] for p in pairs], jnp.int32)
    return pl.pallas_call(
        functools.partial(_fox_prompt_kernel, heads=heads, dh=dh),
        grid_spec=pltpu.PrefetchScalarGridSpec(
            num_scalar_prefetch=2,
            grid=(bsz, len(pairs)),
            in_specs=[
                pl.BlockSpec((None, t, w), lambda b, s, qt, kt: (b, qt[s], 2)),
                pl.BlockSpec((None, t, w), lambda b, s, qt, kt: (b, kt[s], 0)),
                pl.BlockSpec((None, t, w), lambda b, s, qt, kt: (b, kt[s], 1)),
                pl.BlockSpec((None, heads, t, LANES), lambda b, s, qt, kt: (b, 0, qt[s], 0)),
                pl.BlockSpec((None, heads, t), lambda b, s, qt, kt: (b, 0, kt[s])),
            ],
            out_specs=pl.BlockSpec((None, t, w), lambda b, s, qt, kt: (b, qt[s], 0)),
            scratch_shapes=[pltpu.VMEM((heads, t, LANES), F32), pltpu.VMEM((heads, t, LANES), F32),
                            pltpu.VMEM((t, w), F32)],
        ),
        out_shape=jax.ShapeDtypeStruct((bsz, l, w), F32),
        compiler_params=_params("parallel", "arbitrary"),
        name="fox_prompt",
    )(qi_tab, ki_tab, qkv16, qkv16, qkv16, ccol, crow)


def _fox_sample_kernel(pt_ref, q_ref, kn_ref, vn_ref, lfr_ref, lfc_ref, *refs, heads, pp):
    del pt_ref
    k_refs, v_refs, lf_refs = refs[:pp], refs[pp:2 * pp], refs[2 * pp:3 * pp]
    o_ref = refs[3 * pp]
    m_scr, l_scr, acc_scr, rb_scr, carry_scr, lf_scr = refs[3 * pp + 1:]
    g = pl.program_id(1)
    n = q_ref.shape[0]
    w = k_refs[0].shape[0]
    qb = q_ref[...]

    @pl.when(g == 0)
    def _():
        ii = lax.broadcasted_iota(jnp.int32, (n, n), 0)
        jj = lax.broadcasted_iota(jnp.int32, (n, n), 1)
        same_head = (ii & (heads - 1)) == (jj & (heads - 1))
        m_row = jnp.where(same_head & (ii <= jj), 1.0, 0.0).astype(BF16)
        m_col = jnp.where(same_head & (jj <= ii), 1.0, 0.0).astype(BF16)
        cn_row = _dot01_right(jnp.broadcast_to(lfr_ref[...], (8, n)), m_row)[0:1, :] * LOG2E
        cn_col = _dot01(m_col, lfc_ref[...]) * LOG2E
        iw = lax.broadcasted_iota(jnp.int32, (n, w), 0)
        jw = lax.broadcasted_iota(jnp.int32, (n, w), 1)
        rb_scr[...] = jnp.where((iw & (heads - 1)) == (jw & (heads - 1)), cn_col, -jnp.inf)
        carry_scr[...] = jnp.zeros_like(carry_scr)
        s = _dot_nt(qb, kn_ref[...]) + cn_col - cn_row
        s = jnp.where(same_head & (jj <= ii), s, -jnp.inf)
        m0 = jnp.max(s, axis=-1, keepdims=True)
        p = jnp.exp2(s - m0)
        m_scr[...] = m0
        l_scr[...] = jnp.sum(p, axis=-1, keepdims=True)
        acc_scr[...] = _dot(p.astype(BF16), vn_ref[...])

    for j in range(pp):
        lf_scr[j:j + 1, :] = lf_refs[j][...]
    x = lf_scr[...] * LOG2E
    lane = lax.broadcasted_iota(jnp.int32, (pp, w), 1)
    incl, tot = x, x
    d = heads
    while d < w:
        incl = incl + jnp.where(lane < w - d, pltpu.roll(incl, w - d, 1), 0.0)
        tot = tot + pltpu.roll(tot, d, 1)
        d *= 2
    excl = jnp.where(lane < w - heads, pltpu.roll(incl, w - heads, 1), 0.0)
    run = carry_scr[...]
    bias = [None] * pp
    for j in reversed(range(pp)):
        bias[j] = excl[j:j + 1, :] + run
        run = run + tot[j:j + 1, :]
    carry_scr[...] = run

    rb = rb_scr[...]
    s_list = [_dot_nt(qb, k_refs[j][...].astype(BF16)) + rb + bias[j] for j in range(pp)]
    mx = s_list[0]
    for s in s_list[1:]:
        mx = jnp.maximum(mx, s)
    m_old = m_scr[...]
    m_new = jnp.maximum(m_old, jnp.max(mx, axis=-1, keepdims=True))
    alpha = jnp.exp2(m_old - m_new)
    l_new = alpha * l_scr[...]
    acc = alpha * acc_scr[...]
    for j in range(pp):
        p = jnp.exp2(s_list[j] - m_new)
        l_new = l_new + jnp.sum(p, axis=-1, keepdims=True)
        acc = acc + _dot(p.astype(BF16), v_refs[j][...].astype(BF16))
    m_scr[...] = m_new
    l_scr[...] = l_new
    acc_scr[...] = acc

    @pl.when(g == pl.num_programs(1) - 1)
    def _():
        o_ref[...] = acc / l_new


def _fox_sample(q16, k16, v16, lf_row, lf_col, cache_k, cache_v, cache_lf, page_table, layer, *, heads, pp):
    bd, n, dh = q16.shape
    w = cache_k.shape[2]
    n_pages = page_table.shape[1]
    assert n_pages % pp == 0
    steps = n_pages // pp
    tok = lambda b, g, pt: (b, 0, 0)

    def page(j):
        return lambda b, g, pt: (layer, pt[b, n_pages - pp * (g + 1) + j], 0, 0)

    return pl.pallas_call(
        functools.partial(_fox_sample_kernel, heads=heads, pp=pp),
        grid_spec=pltpu.PrefetchScalarGridSpec(
            num_scalar_prefetch=1,
            grid=(bd, steps),
            in_specs=[
                pl.BlockSpec((None, n, dh), tok),
                pl.BlockSpec((None, n, dh), tok),
                pl.BlockSpec((None, n, dh), tok),
                pl.BlockSpec((None, 1, n), tok),
                pl.BlockSpec((None, n, 1), tok),
            ] + [pl.BlockSpec((None, None, w, dh), page(j)) for j in range(pp)]
              + [pl.BlockSpec((None, None, w, dh), page(j)) for j in range(pp)]
              + [pl.BlockSpec((None, None, 1, w), page(j)) for j in range(pp)],
            out_specs=pl.BlockSpec((None, n, dh), tok),
            scratch_shapes=[pltpu.VMEM((n, 1), F32), pltpu.VMEM((n, 1), F32), pltpu.VMEM((n, dh), F32),
                            pltpu.VMEM((n, w), F32), pltpu.VMEM((1, w), F32), pltpu.VMEM((pp, w), F32)],
        ),
        out_shape=jax.ShapeDtypeStruct((bd, n, dh), F32),
        compiler_params=_params("parallel", "arbitrary"),
        name="fox_sample",
    )(page_table, q16, k16, v16, lf_row, lf_col, *([cache_k] * pp), *([cache_v] * pp), *([cache_lf] * pp))


def _merge_kernel(x_ref, oa_ref, ob_ref, za_ref, zb_ref, ga_ref, gb_ref, ng_ref, wa_ref, wb_ref, wo_ref,
                  pg_ref, y_ref, *, heads):
    ng = ng_ref[...]
    parts = []
    for h in range(heads):
        cols = slice(h * LANES, (h + 1) * LANES)
        blk = oa_ref[:, cols]
        nrm = blk * lax.rsqrt(jnp.mean(blk * blk, axis=-1, keepdims=True) + RMS_EPS) * ng
        za = za_ref[:, cols].astype(F32)
        parts.append((nrm * (za * _sigmoid(za))).astype(BF16))
    ya = jnp.concatenate(parts, axis=-1)
    zb = zb_ref[...].astype(F32)
    yb = (ob_ref[...] * (zb * _sigmoid(zb))).astype(BF16)
    u = (_sigmoid(ga_ref[...].astype(F32)) * _dot(ya, wa_ref[...])
         + _sigmoid(gb_ref[...].astype(F32)) * _dot(yb, wb_ref[...]))
    mix = _dot(u.astype(BF16), wo_ref[...])
    nrm = mix * lax.rsqrt(jnp.mean(mix * mix, axis=-1, keepdims=True) + RMS_EPS) * pg_ref[...]
    y_ref[...] = x_ref[...] + nrm


def _merge(x, oa, ob, p16, norm_g, w_a, w_b, w_out, post_g, layer, *, heads, tm_pref):
    m, d = x.shape
    w = oa.shape[1]
    tm = _tile(m, tm_pref)
    assert (P16_G * w) % d == 0
    g_col = (P16_G * w) // d
    row = lambda i: (i, 0)
    lay = lambda i: (layer, 0, 0)
    return pl.pallas_call(
        functools.partial(_merge_kernel, heads=heads),
        grid=(m // tm,),
        in_specs=[
            pl.BlockSpec((tm, d), row),
            pl.BlockSpec((tm, w), row),
            pl.BlockSpec((tm, w), row),
            pl.BlockSpec((tm, w), lambda i: (i, P16_ZA)),
            pl.BlockSpec((tm, w), lambda i: (i, P16_ZB)),
            pl.BlockSpec((tm, d), lambda i: (i, g_col)),
            pl.BlockSpec((tm, d), lambda i: (i, g_col + 1)),
            pl.BlockSpec((None, 1, LANES), lay),
            pl.BlockSpec((None,) + w_a.shape[1:], lay),
            pl.BlockSpec((None,) + w_b.shape[1:], lay),
            pl.BlockSpec((None,) + w_out.shape[1:], lay),
            pl.BlockSpec((None, 1, d), lay),
        ],
        out_specs=pl.BlockSpec((tm, d), row),
        out_shape=jax.ShapeDtypeStruct((m, d), F32),
        compiler_params=_params("parallel"),
        name="merge_out",
    )(x, oa, ob, p16, p16, p16, p16, norm_g, w_a, w_b, w_out, post_g)


def kernel(x_prompt, x_sample, state_hgrn, cache_k, cache_v, cache_logf, page_table, w_in, hgrn_lower_bounds,
           hgrn_norm_g, fox_f_bias, w_branch_a, w_branch_b, w_out, pre_norm_g, post_norm_g):
    bp, seq, d = x_prompt.shape
    bd, tdec, _ = x_sample.shape
    depth, _, heads, dk, dv = state_hgrn.shape
    _, n_pool, page, hb, dh = cache_k.shape
    assert heads == hb and dk == dv == dh == LANES
    w = heads * dk
    n_pages = page_table.shape[1]
    q_scale = LOG2E * dh ** -0.5

    lb_all = _lower_bounds(hgrn_lower_bounds).reshape(depth, heads, dk)

    w_in_t = jnp.swapaxes(w_in, 1, 2)
    w_main = w_in_t[:, :N_MAIN * w, :].astype(BF16)
    w_g = w_in_t[:, N_MAIN * w + heads:, :].astype(BF16)
    w_f = jnp.pad(w_in_t[:, N_MAIN * w:N_MAIN * w + heads, :], ((0, 0), (0, LANES - heads), (0, 0))).astype(BF16)
    f_bias = jnp.pad(fox_f_bias.astype(F32), ((0, 0), (0, LANES - heads)))[:, None, :]
    w_a16 = w_branch_a.astype(BF16)
    w_b16 = w_branch_b.astype(BF16)
    w_o16 = w_out.astype(BF16)
    pre_g = pre_norm_g.astype(F32)[:, None, :]
    post_g = post_norm_g.astype(F32)[:, None, :]
    norm_g = hgrn_norm_g.astype(F32)[:, None, :]

    ck = cache_k.reshape(depth, n_pool, page * heads, dh)
    cv = cache_v.reshape(depth, n_pool, page * heads, dh)
    clf = cache_logf.astype(F32).reshape(depth, n_pool, 1, page * heads)

    mp, ms = bp * seq, bd * tdec
    xp = x_prompt.reshape(mp, d)
    xs = x_sample.reshape(ms, d)
    zeros_state = jnp.zeros((bp, heads, dk, dv), F32)
    assert seq % HGRN_CHUNK == 0 and (tdec & (tdec - 1)) == 0
    kp = jnp.zeros((depth, mp, w), F32)
    vp = jnp.zeros((depth, mp, w), F32)
    ks = jnp.zeros((depth, ms, w), F32)
    vs = jnp.zeros((depth, ms, w), F32)
    nh = tdec * heads

    st_p, st_s, lfp_l, lfs_l = [], [], [], []
    for l in range(depth):
        fa, lf, qkv16, p16, kp, vp = _in_proj(xp, pre_g, w_main, w_g, w_f, f_bias, kp, vp, l,
                                              group_w=w, q_scale=q_scale, tm_pref=TM_PROJ)
        oa, s_fin = _hgrn(p16.reshape(bp, seq, -1), fa.reshape(bp, seq, w), lb_all[l], zeros_state,
                          chunk=HGRN_CHUNK, tb_pref=512)
        ccol, crow = _seq_cumsum(lf.reshape(bp, seq, LANES), heads)
        ob = _fox_prompt(qkv16.reshape(bp, seq, 3 * w), ccol, crow, heads=heads, dh=dh)
        xp = _merge(xp, oa.reshape(mp, w), ob.reshape(mp, w), p16, norm_g, w_a16, w_b16, w_o16, post_g, l,
                    heads=heads, tm_pref=256)
        st_p.append(s_fin)
        lfp_l.append(lf[:, :heads].reshape(bp, seq, heads))

        fa, lf, qkv16, p16, ks, vs = _in_proj(xs, pre_g, w_main, w_g, w_f, f_bias, ks, vs, l,
                                              group_w=w, q_scale=q_scale, tm_pref=TM_PROJ)
        oa, s_fin = _hgrn(p16.reshape(bd, tdec, -1), fa.reshape(bd, tdec, w), lb_all[l],
                          state_hgrn[l].astype(F32), chunk=tdec, tb_pref=tdec)
        lf8 = lf[:, :heads].reshape(bd, nh)
        ob = _fox_sample(qkv16[:, 2 * w:].reshape(bd, nh, dh), qkv16[:, :w].reshape(bd, nh, dh),
                         qkv16[:, w:2 * w].reshape(bd, nh, dh), lf8[:, None, :], lf8[:, :, None],
                         ck, cv, clf, page_table, l, heads=heads, pp=_tile(n_pages, PAGES_PER_STEP))
        xs = _merge(xs, oa.reshape(ms, w), ob.reshape(ms, w), p16, norm_g, w_a16, w_b16, w_o16, post_g, l,
                    heads=heads, tm_pref=256)
        st_s.append(s_fin)
        lfs_l.append(lf[:, :heads].reshape(bd, tdec, heads))

    return (xp.reshape(bp, seq, d), xs.reshape(bd, tdec, d), jnp.stack(st_p), jnp.stack(st_s),
            kp.reshape(depth, bp, seq, heads, dh), vp.reshape(depth, bp, seq, heads, dh), jnp.stack(lfp_l),
            ks.reshape(depth, bd, tdec, heads, dh), vs.reshape(depth, bd, tdec, heads, dh), jnp.stack(lfs_l))
```

```python
import functools

import jax
import jax.numpy as jnp
from jax import lax
from jax.experimental import pallas as pl
from jax.experimental.pallas import tpu as pltpu

F32 = jnp.float32
BF16 = jnp.bfloat16

RMS_EPS = 1e-6
LOG2E = 1.4426950408889634
HGRN_CHUNK = 64
LANES = 128
TM_PROJ = 512
TM_MERGE = 256
HGRN_ROWS = 512
PAGES_PER_STEP = 16
FOX_TILE = 512
FOX_SUB_ROWS = 256
VMEM_LIMIT_BYTES = 60000 * 1024


def _params(*sem):
    return pltpu.CompilerParams(dimension_semantics=sem, vmem_limit_bytes=VMEM_LIMIT_BYTES)


def _tile(n, pref):
    if n <= pref:
        return n
    t = pref
    while n % t:
        t //= 2
    return t


def _dot(a, b):
    return jnp.dot(a, b, preferred_element_type=F32)


def _dot_nt(a, b):
    return lax.dot_general(a, b, (((1,), (1,)), ((), ())), preferred_element_type=F32)


def _dot_tn(a, b):
    return lax.dot_general(a, b, (((0,), (0,)), ((), ())), preferred_element_type=F32)


def _split3(a):
    hi = a.astype(BF16)
    r = a - hi.astype(F32)
    mid = r.astype(BF16)
    lo = (r - mid.astype(F32)).astype(BF16)
    return hi, mid, lo


def _dot01(m01, a):
    hi, mid, lo = _split3(a)
    return _dot(m01, hi) + _dot(m01, mid) + _dot(m01, lo)


def _dot01_right(a, m01):
    hi, mid, lo = _split3(a)
    return _dot(hi, m01) + _dot(mid, m01) + _dot(lo, m01)


def _sigmoid(x):
    return 1.0 / (1.0 + jnp.exp(-x))


def _log_sigmoid(z):
    return jnp.minimum(z, 0.0) - jnp.log(1.0 + jnp.exp(-jnp.abs(z)))


def _lb_kernel(x_ref, o_ref):
    x = x_ref[...]
    depth = x.shape[0]
    e = jnp.exp(x - jnp.max(x, axis=0, keepdims=True))
    sm = e / jnp.sum(e, axis=0, keepdims=True)
    acc = jnp.zeros_like(sm[0:1])
    for i in range(depth):
        o_ref[i:i + 1, :] = acc
        if i + 1 < depth:
            acc = acc + sm[i + 1:i + 2]


def _lower_bounds(lb_param):
    return pl.pallas_call(
        _lb_kernel,
        out_shape=jax.ShapeDtypeStruct(lb_param.shape, F32),
        name="hgrn_lower_bounds",
    )(lb_param.astype(F32))


FA_STEP, KB_STEP, VB_STEP, QB_STEP = 0, 1, 2, 3
MAIN_GROUP_ORDER = (1, 5, 6, 4, 0, 2, 3, 7)
N_MAIN = len(MAIN_GROUP_ORDER)
P16_FIRST_STEP = 4
P16_QA, P16_IA, P16_ZA, P16_ZB, P16_G = 0, 1, 2, 3, 4


def _inproj_kernel(order_ref, x_ref, g_ref, wm_ref, wg_ref, wf_ref, fb_ref, kst_ref, vst_ref,
                   fa_ref, lf_ref, qkv16_ref, p16_ref, k_ref, v_ref, h_scr, *, q_scale):
    del order_ref, kst_ref, vst_ref
    j = pl.program_id(1)

    @pl.when(j == 0)
    def _():
        x = x_ref[...]
        h = x * lax.rsqrt(jnp.mean(x * x, axis=-1, keepdims=True) + RMS_EPS) * g_ref[...]
        hb = h.astype(BF16)
        h_scr[...] = hb
        lf_ref[...] = _log_sigmoid(_dot_nt(hb, wf_ref[...]) + fb_ref[...])

    @pl.when(j == FA_STEP)
    def _():
        fa_ref[...] = _dot_nt(h_scr[...], wm_ref[...])

    @pl.when(j == KB_STEP)
    def _():
        res = _dot_nt(h_scr[...], wm_ref[...])
        k_ref[...] = res
        qkv16_ref[...] = res.astype(BF16)

    @pl.when(j == VB_STEP)
    def _():
        res = _dot_nt(h_scr[...], wm_ref[...])
        v_ref[...] = res
        qkv16_ref[...] = res.astype(BF16)

    @pl.when(j == QB_STEP)
    def _():
        qkv16_ref[...] = (_dot_nt(h_scr[...], wm_ref[...]) * q_scale).astype(BF16)

    @pl.when((j >= P16_FIRST_STEP) & (j < N_MAIN))
    def _():
        p16_ref[...] = _dot_nt(h_scr[...], wm_ref[...]).astype(BF16)

    @pl.when(j >= N_MAIN)
    def _():
        p16_ref[...] = _dot_nt(h_scr[...], wg_ref[...]).astype(BF16)


def _in_proj(x, pre_g, w_main, w_g, w_f, f_bias, k_stack, v_stack, layer, *, group_w, q_scale, tm_pref):
    m, d = x.shape
    tn = group_w
    assert w_main.shape[1] == N_MAIN * tn and w_g.shape[1] % tn == 0
    n_g = w_g.shape[1] // tn
    tm = _tile(m, tm_pref)
    order = jnp.asarray(MAIN_GROUP_ORDER, jnp.int32)
    row = lambda i, j, o: (i, 0)
    return pl.pallas_call(
        functools.partial(_inproj_kernel, q_scale=q_scale),
        grid_spec=pltpu.PrefetchScalarGridSpec(
            num_scalar_prefetch=1,
            grid=(m // tm, N_MAIN + n_g),
            in_specs=[
                pl.BlockSpec((tm, d), row),
                pl.BlockSpec((None, 1, d), lambda i, j, o: (layer, 0, 0)),
                pl.BlockSpec((None, tn, d), lambda i, j, o: (layer, o[jnp.minimum(j, N_MAIN - 1)], 0)),
                pl.BlockSpec((None, tn, d), lambda i, j, o: (layer, jnp.maximum(j - N_MAIN, 0), 0)),
                pl.BlockSpec((None, LANES, d), lambda i, j, o: (layer, 0, 0)),
                pl.BlockSpec((None, 1, LANES), lambda i, j, o: (layer, 0, 0)),
                pl.BlockSpec(memory_space=pl.ANY),
                pl.BlockSpec(memory_space=pl.ANY),
            ],
            out_specs=[
                pl.BlockSpec((tm, tn), row),
                pl.BlockSpec((tm, LANES), row),
                pl.BlockSpec((tm, tn), lambda i, j, o: (i, jnp.clip(j - KB_STEP, 0, 2))),
                pl.BlockSpec((tm, tn), lambda i, j, o: (i, jnp.maximum(j - P16_FIRST_STEP, 0))),
                pl.BlockSpec((None, tm, tn), lambda i, j, o: (layer, i, 0)),
                pl.BlockSpec((None, tm, tn), lambda i, j, o: (layer, i, 0)),
            ],
            scratch_shapes=[pltpu.VMEM((tm, d), BF16)],
        ),
        out_shape=[
            jax.ShapeDtypeStruct((m, tn), F32),
            jax.ShapeDtypeStruct((m, LANES), F32),
            jax.ShapeDtypeStruct((m, 3 * tn), BF16),
            jax.ShapeDtypeStruct((m, (N_MAIN - P16_FIRST_STEP + n_g) * tn), BF16),
            jax.ShapeDtypeStruct(k_stack.shape, F32),
            jax.ShapeDtypeStruct(v_stack.shape, F32),
        ],
        input_output_aliases={7: 4, 8: 5},
        compiler_params=_params("parallel", "arbitrary"),
        name="in_proj",
    )(order, x, pre_g, w_main, w_g, w_f, f_bias, k_stack, v_stack)


def _cumsum_kernel(lf_ref, c_ref, ct_ref, carry):
    @pl.when(pl.program_id(1) == 0)
    def _():
        carry[...] = jnp.zeros_like(carry)

    t = lf_ref.shape[0]
    row = lax.broadcasted_iota(jnp.int32, (t, t), 0)
    col = lax.broadcasted_iota(jnp.int32, (t, t), 1)
    tri = jnp.where(col <= row, 1.0, 0.0).astype(BF16)
    c_nat = _dot01(tri, lf_ref[...]) + carry[...]
    carry[...] = c_nat[t - 1:t, :]
    c = c_nat * LOG2E
    heads = ct_ref.shape[0]
    for h in range(heads):
        c_ref[h] = jnp.broadcast_to(c[:, h:h + 1], c.shape)
    ct_ref[...] = c.T[:heads, :]


def _seq_cumsum(lf, heads):
    b, l, w = lf.shape
    t = _tile(l, 512)
    return pl.pallas_call(
        _cumsum_kernel,
        grid=(b, l // t),
        in_specs=[pl.BlockSpec((None, t, w), lambda i, j: (i, j, 0))],
        out_specs=[pl.BlockSpec((None, heads, t, w), lambda i, j: (i, 0, j, 0)),
                   pl.BlockSpec((None, heads, t), lambda i, j: (i, 0, j))],
        out_shape=[jax.ShapeDtypeStruct((b, heads, l, w), F32), jax.ShapeDtypeStruct((b, heads, l), F32)],
        scratch_shapes=[pltpu.VMEM((1, w), F32)],
        compiler_params=_params("parallel", "arbitrary"),
        name="fox_cumsum",
    )(lf)


def _hgrn_masks(c):
    trow = lax.broadcasted_iota(jnp.int32, (c, LANES), 0)
    ti = lax.broadcasted_iota(jnp.int32, (c, c), 0)
    si = lax.broadcasted_iota(jnp.int32, (c, c), 1)
    tri = jnp.where(si <= ti, 1.0, 0.0).astype(BF16)
    levels = []
    b = 2
    while b <= c:
        half = b // 2
        second = (trow & (b - 1)) >= half
        shift = b.bit_length() - 1
        pair = ((ti >> shift) == (si >> shift)) & ((ti & (b - 1)) >= half) & ((si & (b - 1)) < half)
        levels.append((b, second, pair))
        b *= 2
    return dict(trow=trow, tri=tri, eye=(ti == si), levels=levels)


def _hgrn_ref_rows(g, b, trow):
    c = g.shape[0]
    if b == 2:
        return jnp.where((trow & 1) == 1, pltpu.roll(g, 1, 0), g)
    if b == 4:
        m = trow & 3
        return jnp.where(m == 0, pltpu.roll(g, c - 1, 0),
                         jnp.where(m == 1, g,
                                   jnp.where(m == 2, pltpu.roll(g, 1, 0), pltpu.roll(g, 2, 0))))
    half = b // 2
    g3 = g.reshape(c // b, b, LANES)
    ref = jnp.broadcast_to(g3[:, half - 1:half, :], (c // b, b, LANES))
    return ref.reshape(c, LANES)


def _hgrn_chunk(qs, xs, vs, lbs, sts, masks):
    n = len(qs)
    c = qs[0].shape[0]
    kk, qf, g = [], [], []
    for h in range(n):
        one_m_lb = 1.0 - lbs[h]
        logf = jnp.log(lbs[h] + one_m_lb * _sigmoid(xs[h]))
        kk.append(one_m_lb * _sigmoid(-xs[h]))
        qf.append(qs[h] * _sigmoid(qs[h]))
        g.append(_dot01(masks["tri"], logf))
    vb = [v.astype(BF16) for v in vs]
    o = [_dot_nt((qf[h] * jnp.exp(g[h])).astype(BF16), sts[h].astype(BF16)) for h in range(n)]
    a = [jnp.where(masks["eye"], _dot_nt(qf[h].astype(BF16), kk[h].astype(BF16)), 0.0) for h in range(n)]
    for b, second, pair in masks["levels"]:
        for h in range(n):
            e = jnp.exp(-jnp.abs(g[h] - _hgrn_ref_rows(g[h], b, masks["trow"])))
            qb = jnp.where(second, qf[h] * e, 0.0).astype(BF16)
            kb = jnp.where(second, 0.0, kk[h] * e).astype(BF16)
            a[h] = jnp.where(pair, _dot_nt(qb, kb), a[h])
    o = [o[h] + _dot(a[h].astype(BF16), vb[h]) for h in range(n)]
    st_new = []
    for h in range(n):
        g_end = g[h][c - 1:c, :]
        ke = (kk[h] * jnp.exp(g_end - g[h])).astype(BF16)
        st_new.append(sts[h] * jnp.exp(g_end) + _dot_tn(vb[h], ke))
    return o, st_new


def _hgrn_kernel(q_ref, f_ref, i_ref, lb_ref, s0_ref, o_ref, sfin_ref, st_scr, *, chunk, heads):
    step = pl.program_id(1)
    n_chunks = q_ref.shape[0] // chunk

    @pl.when(step == 0)
    def _():
        for h in range(heads):
            st_scr[h] = s0_ref[h].T

    masks = _hgrn_masks(chunk)

    def body(ci, carry):
        rows = pl.ds(pl.multiple_of(ci * chunk, chunk), chunk)
        cols = [slice(h * LANES, (h + 1) * LANES) for h in range(heads)]
        o, st_new = _hgrn_chunk([q_ref[rows, cl].astype(F32) for cl in cols], [f_ref[rows, cl] for cl in cols],
                                [i_ref[rows, cl].astype(F32) for cl in cols], [lb_ref[h:h + 1, :] for h in range(heads)],
                                [st_scr[h] for h in range(heads)], masks)
        for h in range(heads):
            o_ref[rows, cols[h]] = o[h]
            st_scr[h] = st_new[h]
        return carry

    lax.fori_loop(0, n_chunks, body, 0)

    @pl.when(step == pl.num_programs(1) - 1)
    def _():
        for h in range(heads):
            sfin_ref[h] = st_scr[h].T


def _hgrn(p16, fa, lb, s0, *, chunk, tb_pref):
    bsz, l, _ = p16.shape
    heads, dk = lb.shape
    w = heads * dk
    tb = _tile(l, tb_pref)
    kern = functools.partial(_hgrn_kernel, chunk=chunk, heads=heads)
    return pl.pallas_call(
        kern,
        grid=(bsz, l // tb),
        in_specs=[
            pl.BlockSpec((None, tb, w), lambda b, t: (b, t, P16_QA)),
            pl.BlockSpec((None, tb, w), lambda b, t: (b, t, 0)),
            pl.BlockSpec((None, tb, w), lambda b, t: (b, t, P16_IA)),
            pl.BlockSpec((heads, dk), lambda b, t: (0, 0)),
            pl.BlockSpec((None, heads, dk, dk), lambda b, t: (b, 0, 0, 0)),
        ],
        out_specs=[
            pl.BlockSpec((None, tb, w), lambda b, t: (b, t, 0)),
            pl.BlockSpec((None, heads, dk, dk), lambda b, t: (b, 0, 0, 0)),
        ],
        out_shape=[
            jax.ShapeDtypeStruct((bsz, l, w), F32),
            jax.ShapeDtypeStruct(s0.shape, F32),
        ],
        scratch_shapes=[pltpu.VMEM((heads, dk, dk), F32)],
        compiler_params=_params("parallel", "arbitrary"),
        name="hgrn2",
    )(p16, fa, p16, lb, s0)


def _fox_prompt_kernel(qi_tab, ki_tab, q_ref, k_ref, v_ref, ccol_ref, crow_ref, o_ref, m_scr, l_scr, acc_scr,
                       *, heads, dh):
    step = pl.program_id(1)
    qi = qi_tab[step]
    ki = ki_tab[step]
    tq, tk = q_ref.shape[0], k_ref.shape[0]

    @pl.when(ki == 0)
    def _():
        m_scr[...] = jnp.full_like(m_scr, -jnp.inf)
        l_scr[...] = jnp.zeros_like(l_scr)
        acc_scr[...] = jnp.zeros_like(acc_scr)

    def block(diagonal):
        sub = min(tq, FOX_SUB_ROWS)
        units = [(h, r0) for h in range(heads) for r0 in range(0, tq, sub)]

        def scores(h, r0):
            kw = r0 + sub if diagonal else tk
            return _dot_nt(q_ref[r0:r0 + sub, h * dh:(h + 1) * dh], k_ref[:kw, h * dh:(h + 1) * dh])

        s_next = scores(*units[0])
        for u, (h, r0) in enumerate(units):
            s = s_next
            if u + 1 < len(units):
                s_next = scores(*units[u + 1])
            rows = slice(r0, r0 + sub)
            cols = slice(h * dh, (h + 1) * dh)
            kw = s.shape[1]
            s = s - crow_ref[h:h + 1, :kw]
            if diagonal:
                row = lax.broadcasted_iota(jnp.int32, (sub, kw), 0) + r0
                col = lax.broadcasted_iota(jnp.int32, (sub, kw), 1)
                s = jnp.where(col <= row, s, -jnp.inf)
            cc = ccol_ref[h, rows, :]
            m_old = m_scr[h, rows, :]
            m_new = jnp.maximum(m_old, jnp.max(s, axis=-1, keepdims=True) + cc)
            p = jnp.exp2(s - jnp.concatenate([m_new - cc] * (kw // LANES), axis=1))
            alpha = jnp.exp2(m_old - m_new)
            l_new = alpha * l_scr[h, rows, :] + jnp.sum(p, axis=-1, keepdims=True)
            acc = alpha * acc_scr[rows, cols] + _dot(p.astype(BF16), v_ref[:kw, cols])
            m_scr[h, rows, :] = m_new
            if diagonal:
                o_ref[rows, cols] = acc / l_new
            else:
                l_scr[h, rows, :] = l_new
                acc_scr[rows, cols] = acc

    @pl.when(ki < qi)
    def _():
        block(False)

    @pl.when(ki == qi)
    def _():
        block(True)


def _fox_prompt(qkv16, ccol, crow, *, heads, dh):
    assert dh == LANES
    bsz, l, _ = qkv16.shape
    w = heads * dh
    t = _tile(l, FOX_TILE)
    nq = l // t
    pairs = [(qi, ki) for qi in range(nq) for ki in range(qi + 1)]
    qi_tab = jnp.asarray([p[0] for p in pairs], jnp.int32)
    ki_tab = jnp.asarray([p[1] for p in pairs], jnp.int32)
    return pl.pallas_call(
        functools.partial(_fox_prompt_kernel, heads=heads, dh=dh),
        grid_spec=pltpu.PrefetchScalarGridSpec(
            num_scalar_prefetch=2,
            grid=(bsz, len(pairs)),
            in_specs=[
                pl.BlockSpec((None, t, w), lambda b, s, qt, kt: (b, qt[s], 2)),
                pl.BlockSpec((None, t, w), lambda b, s, qt, kt: (b, kt[s], 0)),
                pl.BlockSpec((None, t, w), lambda b, s, qt, kt: (b, kt[s], 1)),
                pl.BlockSpec((None, heads, t, LANES), lambda b, s, qt, kt: (b, 0, qt[s], 0)),
                pl.BlockSpec((None, heads, t), lambda b, s, qt, kt: (b, 0, kt[s])),
            ],
            out_specs=pl.BlockSpec((None, t, w), lambda b, s, qt, kt: (b, qt[s], 0)),
            scratch_shapes=[pltpu.VMEM((heads, t, LANES), F32), pltpu.VMEM((heads, t, LANES), F32),
                            pltpu.VMEM((t, w), F32)],
        ),
        out_shape=jax.ShapeDtypeStruct((bsz, l, w), F32),
        compiler_params=_params("parallel", "arbitrary"),
        name="fox_prompt",
    )(qi_tab, ki_tab, qkv16, qkv16, qkv16, ccol, crow)


def _fox_sample_kernel(pt_ref, q_ref, kn_ref, vn_ref, lfr_ref, lfc_ref, *refs, heads, pp):
    del pt_ref
    k_refs, v_refs, lf_refs = refs[:pp], refs[pp:2 * pp], refs[2 * pp:3 * pp]
    o_ref = refs[3 * pp]
    m_scr, l_scr, acc_scr, rb_scr, carry_scr, lf_scr = refs[3 * pp + 1:]
    g = pl.program_id(1)
    n = q_ref.shape[0]
    w = k_refs[0].shape[0]
    qb = q_ref[...]

    @pl.when(g == 0)
    def _():
        ii = lax.broadcasted_iota(jnp.int32, (n, n), 0)
        jj = lax.broadcasted_iota(jnp.int32, (n, n), 1)
        same_head = (ii & (heads - 1)) == (jj & (heads - 1))
        m_row = jnp.where(same_head & (ii <= jj), 1.0, 0.0).astype(BF16)
        m_col = jnp.where(same_head & (jj <= ii), 1.0, 0.0).astype(BF16)
        cn_row = _dot01_right(jnp.broadcast_to(lfr_ref[...], (8, n)), m_row)[0:1, :] * LOG2E
        cn_col = _dot01(m_col, lfc_ref[...]) * LOG2E
        iw = lax.broadcasted_iota(jnp.int32, (n, w), 0)
        jw = lax.broadcasted_iota(jnp.int32, (n, w), 1)
        rb_scr[...] = jnp.where((iw & (heads - 1)) == (jw & (heads - 1)), cn_col, -jnp.inf)
        carry_scr[...] = jnp.zeros_like(carry_scr)
        s = _dot_nt(qb, kn_ref[...]) + cn_col - cn_row
        s = jnp.where(same_head & (jj <= ii), s, -jnp.inf)
        m0 = jnp.max(s, axis=-1, keepdims=True)
        p = jnp.exp2(s - m0)
        m_scr[...] = m0
        l_scr[...] = jnp.sum(p, axis=-1, keepdims=True)
        acc_scr[...] = _dot(p.astype(BF16), vn_ref[...])

    for j in range(pp):
        lf_scr[j:j + 1, :] = lf_refs[j][...]
    x = lf_scr[...] * LOG2E
    lane = lax.broadcasted_iota(jnp.int32, (pp, w), 1)
    incl, tot = x, x
    d = heads
    while d < w:
        incl = incl + jnp.where(lane < w - d, pltpu.roll(incl, w - d, 1), 0.0)
        tot = tot + pltpu.roll(tot, d, 1)
        d *= 2
    excl = jnp.where(lane < w - heads, pltpu.roll(incl, w - heads, 1), 0.0)
    run = carry_scr[...]
    bias = [None] * pp
    for j in reversed(range(pp)):
        bias[j] = excl[j:j + 1, :] + run
        run = run + tot[j:j + 1, :]
    carry_scr[...] = run

    rb = rb_scr[...]
    s_list = [_dot_nt(qb, k_refs[j][...].astype(BF16)) + rb + bias[j] for j in range(pp)]
    mx = s_list[0]
    for s in s_list[1:]:
        mx = jnp.maximum(mx, s)
    m_old = m_scr[...]
    m_new = jnp.maximum(m_old, jnp.max(mx, axis=-1, keepdims=True))
    alpha = jnp.exp2(m_old - m_new)
    l_new = alpha * l_scr[...]
    acc = alpha * acc_scr[...]
    for j in range(pp):
        p = jnp.exp2(s_list[j] - m_new)
        l_new = l_new + jnp.sum(p, axis=-1, keepdims=True)
        acc = acc + _dot(p.astype(BF16), v_refs[j][...].astype(BF16))
    m_scr[...] = m_new
    l_scr[...] = l_new
    acc_scr[...] = acc

    @pl.when(g == pl.num_programs(1) - 1)
    def _():
        o_ref[...] = acc / l_new


def _fox_sample(q16, k16, v16, lf_row, lf_col, cache_k, cache_v, cache_lf, page_table, layer, *, heads, pp):
    bd, n, dh = q16.shape
    w = cache_k.shape[2]
    n_pages = page_table.shape[1]
    assert n_pages % pp == 0
    steps = n_pages // pp
    tok = lambda b, g, pt: (b, 0, 0)

    def page(j):
        return lambda b, g, pt: (layer, pt[b, n_pages - pp * (g + 1) + j], 0, 0)

    return pl.pallas_call(
        functools.partial(_fox_sample_kernel, heads=heads, pp=pp),
        grid_spec=pltpu.PrefetchScalarGridSpec(
            num_scalar_prefetch=1,
            grid=(bd, steps),
            in_specs=[
                pl.BlockSpec((None, n, dh), tok),
                pl.BlockSpec((None, n, dh), tok),
                pl.BlockSpec((None, n, dh), tok),
                pl.BlockSpec((None, 1, n), tok),
                pl.BlockSpec((None, n, 1), tok),
            ] + [pl.BlockSpec((None, None, w, dh), page(j)) for j in range(pp)]
              + [pl.BlockSpec((None, None, w, dh), page(j)) for j in range(pp)]
              + [pl.BlockSpec((None, None, 1, w), page(j)) for j in range(pp)],
            out_specs=pl.BlockSpec((None, n, dh), tok),
            scratch_shapes=[pltpu.VMEM((n, 1), F32), pltpu.VMEM((n, 1), F32), pltpu.VMEM((n, dh), F32),
                            pltpu.VMEM((n, w), F32), pltpu.VMEM((1, w), F32), pltpu.VMEM((pp, w), F32)],
        ),
        out_shape=jax.ShapeDtypeStruct((bd, n, dh), F32),
        compiler_params=_params("parallel", "arbitrary"),
        name="fox_sample",
    )(page_table, q16, k16, v16, lf_row, lf_col, *([cache_k] * pp), *([cache_v] * pp), *([cache_lf] * pp))


def _merge_kernel(x_ref, oa_ref, ob_ref, za_ref, zb_ref, ga_ref, gb_ref, ng_ref, wa_ref, wb_ref, wo_ref,
                  pg_ref, y_ref, *, heads):
    ng = ng_ref[...]
    parts = []
    for h in range(heads):
        cols = slice(h * LANES, (h + 1) * LANES)
        blk = oa_ref[:, cols]
        nrm = blk * lax.rsqrt(jnp.mean(blk * blk, axis=-1, keepdims=True) + RMS_EPS) * ng
        za = za_ref[:, cols].astype(F32)
        parts.append((nrm * (za * _sigmoid(za))).astype(BF16))
    ya = jnp.concatenate(parts, axis=-1)
    zb = zb_ref[...].astype(F32)
    yb = (ob_ref[...] * (zb * _sigmoid(zb))).astype(BF16)
    u = (_sigmoid(ga_ref[...].astype(F32)) * _dot(ya, wa_ref[...])
         + _sigmoid(gb_ref[...].astype(F32)) * _dot(yb, wb_ref[...]))
    mix = _dot(u.astype(BF16), wo_ref[...])
    nrm = mix * lax.rsqrt(jnp.mean(mix * mix, axis=-1, keepdims=True) + RMS_EPS) * pg_ref[...]
    y_ref[...] = x_ref[...] + nrm


def _merge(x, oa, ob, p16, norm_g, w_a, w_b, w_out, post_g, layer, *, heads, tm_pref):
    m, d = x.shape
    w = oa.shape[1]
    tm = _tile(m, tm_pref)
    assert (P16_G * w) % d == 0
    g_col = (P16_G * w) // d
    row = lambda i: (i, 0)
    lay = lambda i: (layer, 0, 0)
    return pl.pallas_call(
        functools.partial(_merge_kernel, heads=heads),
        grid=(m // tm,),
        in_specs=[
            pl.BlockSpec((tm, d), row),
            pl.BlockSpec((tm, w), row),
            pl.BlockSpec((tm, w), row),
            pl.BlockSpec((tm, w), lambda i: (i, P16_ZA)),
            pl.BlockSpec((tm, w), lambda i: (i, P16_ZB)),
            pl.BlockSpec((tm, d), lambda i: (i, g_col)),
            pl.BlockSpec((tm, d), lambda i: (i, g_col + 1)),
            pl.BlockSpec((None, 1, LANES), lay),
            pl.BlockSpec((None,) + w_a.shape[1:], lay),
            pl.BlockSpec((None,) + w_b.shape[1:], lay),
            pl.BlockSpec((None,) + w_out.shape[1:], lay),
            pl.BlockSpec((None, 1, d), lay),
        ],
        out_specs=pl.BlockSpec((tm, d), row),
        out_shape=jax.ShapeDtypeStruct((m, d), F32),
        compiler_params=_params("parallel"),
        name="merge_out",
    )(x, oa, ob, p16, p16, p16, p16, norm_g, w_a, w_b, w_out, post_g)


def kernel(x_prompt, x_sample, state_hgrn, cache_k, cache_v, cache_logf, page_table, w_in, hgrn_lower_bounds,
           hgrn_norm_g, fox_f_bias, w_branch_a, w_branch_b, w_out, pre_norm_g, post_norm_g):
    bp, seq, d = x_prompt.shape
    bd, tdec, _ = x_sample.shape
    depth, _, heads, dk, dv = state_hgrn.shape
    _, n_pool, page, hb, dh = cache_k.shape
    assert heads == hb and dk == dv == dh == LANES
    w = heads * dk
    n_pages = page_table.shape[1]
    q_scale = LOG2E * dh ** -0.5

    lb_all = _lower_bounds(hgrn_lower_bounds).reshape(depth, heads, dk)

    w_in_t = jnp.swapaxes(w_in, 1, 2)
    w_main = w_in_t[:, :N_MAIN * w, :].astype(BF16)
    w_g = w_in_t[:, N_MAIN * w + heads:, :].astype(BF16)
    w_f = jnp.pad(w_in_t[:, N_MAIN * w:N_MAIN * w + heads, :], ((0, 0), (0, LANES - heads), (0, 0))).astype(BF16)
    f_bias = jnp.pad(fox_f_bias.astype(F32), ((0, 0), (0, LANES - heads)))[:, None, :]
    w_a16 = w_branch_a.astype(BF16)
    w_b16 = w_branch_b.astype(BF16)
    w_o16 = w_out.astype(BF16)
    pre_g = pre_norm_g.astype(F32)[:, None, :]
    post_g = post_norm_g.astype(F32)[:, None, :]
    norm_g = hgrn_norm_g.astype(F32)[:, None, :]

    ck = cache_k.reshape(depth, n_pool, page * heads, dh)
    cv = cache_v.reshape(depth, n_pool, page * heads, dh)
    clf = cache_logf.astype(F32).reshape(depth, n_pool, 1, page * heads)

    mp, ms = bp * seq, bd * tdec
    xp = x_prompt.reshape(mp, d)
    xs = x_sample.reshape(ms, d)
    zeros_state = jnp.zeros((bp, heads, dk, dv), F32)
    assert seq % HGRN_CHUNK == 0 and (tdec & (tdec - 1)) == 0
    kp = lax.empty((depth, mp, w), F32)
    vp = lax.empty((depth, mp, w), F32)
    ks = lax.empty((depth, ms, w), F32)
    vs = lax.empty((depth, ms, w), F32)
    nh = tdec * heads

    st_p, st_s, lfp_l, lfs_l = [], [], [], []
    for l in range(depth):
        fa, lf, qkv16, p16, kp, vp = _in_proj(xp, pre_g, w_main, w_g, w_f, f_bias, kp, vp, l,
                                              group_w=w, q_scale=q_scale, tm_pref=TM_PROJ)
        oa, s_fin = _hgrn(p16.reshape(bp, seq, -1), fa.reshape(bp, seq, w), lb_all[l], zeros_state,
                          chunk=HGRN_CHUNK, tb_pref=HGRN_ROWS)
        ccol, crow = _seq_cumsum(lf.reshape(bp, seq, LANES), heads)
        ob = _fox_prompt(qkv16.reshape(bp, seq, 3 * w), ccol, crow, heads=heads, dh=dh)
        xp = _merge(xp, oa.reshape(mp, w), ob.reshape(mp, w), p16, norm_g, w_a16, w_b16, w_o16, post_g, l,
                    heads=heads, tm_pref=TM_MERGE)
        st_p.append(s_fin)
        lfp_l.append(lf[:, :heads].reshape(bp, seq, heads))

        fa, lf, qkv16, p16, ks, vs = _in_proj(xs, pre_g, w_main, w_g, w_f, f_bias, ks, vs, l,
                                              group_w=w, q_scale=q_scale, tm_pref=TM_PROJ)
        oa, s_fin = _hgrn(p16.reshape(bd, tdec, -1), fa.reshape(bd, tdec, w), lb_all[l],
                          state_hgrn[l].astype(F32), chunk=tdec, tb_pref=tdec)
        lf8 = lf[:, :heads].reshape(bd, nh)
        ob = _fox_sample(qkv16[:, 2 * w:].reshape(bd, nh, dh), qkv16[:, :w].reshape(bd, nh, dh),
                         qkv16[:, w:2 * w].reshape(bd, nh, dh), lf8[:, None, :], lf8[:, :, None],
                         ck, cv, clf, page_table, l, heads=heads, pp=_tile(n_pages, PAGES_PER_STEP))
        xs = _merge(xs, oa.reshape(ms, w), ob.reshape(ms, w), p16, norm_g, w_a16, w_b16, w_o16, post_g, l,
                    heads=heads, tm_pref=TM_MERGE)
        st_s.append(s_fin)
        lfs_l.append(lf[:, :heads].reshape(bd, tdec, heads))

    return (xp.reshape(bp, seq, d), xs.reshape(bd, tdec, d), jnp.stack(st_p), jnp.stack(st_s),
            kp.reshape(depth, bp, seq, heads, dh), vp.reshape(depth, bp, seq, heads, dh), jnp.stack(lfp_l),
            ks.reshape(depth, bd, tdec, heads, dh), vs.reshape(depth, bd, tdec, heads, dh), jnp.stack(lfs_l))
```

```python
import functools

import jax
import jax.numpy as jnp
from jax import lax
from jax.experimental import pallas as pl
from jax.experimental.pallas import tpu as pltpu

F32 = jnp.float32
BF16 = jnp.bfloat16

RMS_EPS = 1e-6
LOG2E = 1.4426950408889634
HGRN_CHUNK = 64
LANES = 128
TM_PROJ = 512
TM_PROJ_ACT = 1024
TM_MERGE = 256
HGRN_ROWS = 512
PAGES_PER_STEP = 16
FOX_TILE = 512
FOX_SUB_ROWS = 256
VMEM_LIMIT_BYTES = 60000 * 1024


def _params(*sem):
    return pltpu.CompilerParams(dimension_semantics=sem, vmem_limit_bytes=VMEM_LIMIT_BYTES)


def _tile(n, pref):
    if n <= pref:
        return n
    t = pref
    while n % t:
        t //= 2
    return t


def _dot(a, b):
    return jnp.dot(a, b, preferred_element_type=F32)


def _dot_nt(a, b):
    return lax.dot_general(a, b, (((1,), (1,)), ((), ())), preferred_element_type=F32)


def _dot_tn(a, b):
    return lax.dot_general(a, b, (((0,), (0,)), ((), ())), preferred_element_type=F32)


def _split3(a):
    hi = a.astype(BF16)
    r = a - hi.astype(F32)
    mid = r.astype(BF16)
    lo = (r - mid.astype(F32)).astype(BF16)
    return hi, mid, lo


def _dot01(m01, a):
    hi, mid, lo = _split3(a)
    return _dot(m01, hi) + _dot(m01, mid) + _dot(m01, lo)


def _dot01_right(a, m01):
    hi, mid, lo = _split3(a)
    return _dot(hi, m01) + _dot(mid, m01) + _dot(lo, m01)


def _sigmoid(x):
    return 1.0 / (1.0 + jnp.exp(-x))


def _log_sigmoid(z):
    return jnp.minimum(z, 0.0) - jnp.log(1.0 + jnp.exp(-jnp.abs(z)))


def _lb_kernel(x_ref, o_ref):
    x = x_ref[...]
    depth = x.shape[0]
    e = jnp.exp(x - jnp.max(x, axis=0, keepdims=True))
    sm = e / jnp.sum(e, axis=0, keepdims=True)
    acc = jnp.zeros_like(sm[0:1])
    for i in range(depth):
        o_ref[i:i + 1, :] = acc
        if i + 1 < depth:
            acc = acc + sm[i + 1:i + 2]


def _lower_bounds(lb_param):
    return pl.pallas_call(
        _lb_kernel,
        out_shape=jax.ShapeDtypeStruct(lb_param.shape, F32),
        name="hgrn_lower_bounds",
    )(lb_param.astype(F32))


FA_STEP, KB_STEP, VB_STEP, QB_STEP = 0, 1, 2, 3
MAIN_GROUP_ORDER = (1, 5, 6, 4, 0, 2, 3, 7)
N_MAIN = len(MAIN_GROUP_ORDER)
P16_FIRST_STEP = 4
P16_QA, P16_IA, P16_ZA, P16_ZB, P16_G = 0, 1, 2, 3, 4


N_ACT_MAIN = N_MAIN - P16_FIRST_STEP


def _inproj_norm(x_ref, g_ref, h_scr):
    x = x_ref[...]
    h = x * lax.rsqrt(jnp.mean(x * x, axis=-1, keepdims=True) + RMS_EPS) * g_ref[...]
    hb = h.astype(BF16)
    h_scr[...] = hb
    return hb


def _inproj_state_kernel(order_ref, x_ref, g_ref, wm_ref, wf_ref, fb_ref, kst_ref, vst_ref,
                         fa_ref, lf_ref, qkv16_ref, k_ref, v_ref, h_scr, *, q_scale):
    del order_ref, kst_ref, vst_ref
    j = pl.program_id(1)

    @pl.when(j == 0)
    def _():
        hb = _inproj_norm(x_ref, g_ref, h_scr)
        lf_ref[...] = _log_sigmoid(_dot_nt(hb, wf_ref[...]) + fb_ref[...])

    @pl.when(j == FA_STEP)
    def _():
        fa_ref[...] = _dot_nt(h_scr[...], wm_ref[...])

    @pl.when(j == KB_STEP)
    def _():
        res = _dot_nt(h_scr[...], wm_ref[...])
        k_ref[...] = res
        qkv16_ref[...] = res.astype(BF16)

    @pl.when(j == VB_STEP)
    def _():
        res = _dot_nt(h_scr[...], wm_ref[...])
        v_ref[...] = res
        qkv16_ref[...] = res.astype(BF16)

    @pl.when(j == QB_STEP)
    def _():
        qkv16_ref[...] = (_dot_nt(h_scr[...], wm_ref[...]) * q_scale).astype(BF16)

def _inproj_act_kernel(order_ref, x_ref, g_ref, wm_ref, wg_ref, p16_ref, h_scr):
    del order_ref
    j = pl.program_id(1)

    @pl.when(j == 0)
    def _():
        _inproj_norm(x_ref, g_ref, h_scr)

    @pl.when(j < N_ACT_MAIN)
    def _():
        p16_ref[...] = _dot_nt(h_scr[...], wm_ref[...]).astype(BF16)

    @pl.when(j >= N_ACT_MAIN)
    def _():
        p16_ref[...] = _dot_nt(h_scr[...], wg_ref[...]).astype(BF16)


def _in_proj(x, pre_g, w_main, w_g, w_f, f_bias, k_stack, v_stack, layer, *, group_w, q_scale):
    m, d = x.shape
    tn = group_w
    assert w_main.shape[1] == N_MAIN * tn and w_g.shape[1] % tn == 0
    n_g = w_g.shape[1] // tn
    order = jnp.asarray(MAIN_GROUP_ORDER, jnp.int32)
    row = lambda i, j, o: (i, 0)
    gain = pl.BlockSpec((None, 1, d), lambda i, j, o: (layer, 0, 0))

    tm = _tile(m, TM_PROJ)
    fa, lf, qkv16, k_stack, v_stack = pl.pallas_call(
        functools.partial(_inproj_state_kernel, q_scale=q_scale),
        grid_spec=pltpu.PrefetchScalarGridSpec(
            num_scalar_prefetch=1,
            grid=(m // tm, P16_FIRST_STEP),
            in_specs=[
                pl.BlockSpec((tm, d), row),
                gain,
                pl.BlockSpec((None, tn, d), lambda i, j, o: (layer, o[j], 0)),
                pl.BlockSpec((None, LANES, d), lambda i, j, o: (layer, 0, 0)),
                pl.BlockSpec((None, 1, LANES), lambda i, j, o: (layer, 0, 0)),
                pl.BlockSpec(memory_space=pl.ANY),
                pl.BlockSpec(memory_space=pl.ANY),
            ],
            out_specs=[
                pl.BlockSpec((tm, tn), row),
                pl.BlockSpec((tm, LANES), row),
                pl.BlockSpec((tm, tn), lambda i, j, o: (i, jnp.clip(j - KB_STEP, 0, 2))),
                pl.BlockSpec((None, tm, tn), lambda i, j, o: (layer, i, 0)),
                pl.BlockSpec((None, tm, tn), lambda i, j, o: (layer, i, 0)),
            ],
            scratch_shapes=[pltpu.VMEM((tm, d), BF16)],
        ),
        out_shape=[
            jax.ShapeDtypeStruct((m, tn), F32),
            jax.ShapeDtypeStruct((m, LANES), F32),
            jax.ShapeDtypeStruct((m, 3 * tn), BF16),
            jax.ShapeDtypeStruct(k_stack.shape, F32),
            jax.ShapeDtypeStruct(v_stack.shape, F32),
        ],
        input_output_aliases={6: 3, 7: 4},
        compiler_params=_params("parallel", "arbitrary"),
        name="in_proj_state",
    )(order, x, pre_g, w_main, w_f, f_bias, k_stack, v_stack)

    tm = _tile(m, TM_PROJ_ACT)
    p16 = pl.pallas_call(
        _inproj_act_kernel,
        grid_spec=pltpu.PrefetchScalarGridSpec(
            num_scalar_prefetch=1,
            grid=(m // tm, N_ACT_MAIN + n_g),
            in_specs=[
                pl.BlockSpec((tm, d), row),
                gain,
                pl.BlockSpec((None, tn, d),
                             lambda i, j, o: (layer, o[P16_FIRST_STEP + jnp.minimum(j, N_ACT_MAIN - 1)], 0)),
                pl.BlockSpec((None, tn, d), lambda i, j, o: (layer, jnp.maximum(j - N_ACT_MAIN, 0), 0)),
            ],
            out_specs=pl.BlockSpec((tm, tn), lambda i, j, o: (i, j)),
            scratch_shapes=[pltpu.VMEM((tm, d), BF16)],
        ),
        out_shape=jax.ShapeDtypeStruct((m, (N_ACT_MAIN + n_g) * tn), BF16),
        compiler_params=_params("parallel", "arbitrary"),
        name="in_proj_act",
    )(order, x, pre_g, w_main, w_g)
    return fa, lf, qkv16, p16, k_stack, v_stack


def _cumsum_kernel(lf_ref, c_ref, ct_ref, carry):
    @pl.when(pl.program_id(1) == 0)
    def _():
        carry[...] = jnp.zeros_like(carry)

    t = lf_ref.shape[0]
    row = lax.broadcasted_iota(jnp.int32, (t, t), 0)
    col = lax.broadcasted_iota(jnp.int32, (t, t), 1)
    tri = jnp.where(col <= row, 1.0, 0.0).astype(BF16)
    c_nat = _dot01(tri, lf_ref[...]) + carry[...]
    carry[...] = c_nat[t - 1:t, :]
    c = c_nat * LOG2E
    heads = ct_ref.shape[0]
    for h in range(heads):
        c_ref[h] = jnp.broadcast_to(c[:, h:h + 1], c.shape)
    ct_ref[...] = c.T[:heads, :]


def _seq_cumsum(lf, heads):
    b, l, w = lf.shape
    t = _tile(l, 512)
    return pl.pallas_call(
        _cumsum_kernel,
        grid=(b, l // t),
        in_specs=[pl.BlockSpec((None, t, w), lambda i, j: (i, j, 0))],
        out_specs=[pl.BlockSpec((None, heads, t, w), lambda i, j: (i, 0, j, 0)),
                   pl.BlockSpec((None, heads, t), lambda i, j: (i, 0, j))],
        out_shape=[jax.ShapeDtypeStruct((b, heads, l, w), F32), jax.ShapeDtypeStruct((b, heads, l), F32)],
        scratch_shapes=[pltpu.VMEM((1, w), F32)],
        compiler_params=_params("parallel", "arbitrary"),
        name="fox_cumsum",
    )(lf)


def _hgrn_masks(c):
    trow = lax.broadcasted_iota(jnp.int32, (c, LANES), 0)
    ti = lax.broadcasted_iota(jnp.int32, (c, c), 0)
    si = lax.broadcasted_iota(jnp.int32, (c, c), 1)
    tri = jnp.where(si <= ti, 1.0, 0.0).astype(BF16)
    levels = []
    b = 2
    while b <= c:
        half = b // 2
        second = (trow & (b - 1)) >= half
        shift = b.bit_length() - 1
        pair = ((ti >> shift) == (si >> shift)) & ((ti & (b - 1)) >= half) & ((si & (b - 1)) < half)
        levels.append((b, second, pair))
        b *= 2
    return dict(trow=trow, tri=tri, eye=(ti == si), levels=levels)


def _hgrn_ref_rows(g, b, trow):
    c = g.shape[0]
    if b == 2:
        return jnp.where((trow & 1) == 1, pltpu.roll(g, 1, 0), g)
    if b == 4:
        m = trow & 3
        return jnp.where(m == 0, pltpu.roll(g, c - 1, 0),
                         jnp.where(m == 1, g,
                                   jnp.where(m == 2, pltpu.roll(g, 1, 0), pltpu.roll(g, 2, 0))))
    half = b // 2
    g3 = g.reshape(c // b, b, LANES)
    ref = jnp.broadcast_to(g3[:, half - 1:half, :], (c // b, b, LANES))
    return ref.reshape(c, LANES)


def _hgrn_chunk(qs, xs, vs, lbs, sts, masks):
    n = len(qs)
    c = qs[0].shape[0]
    kk, qf, g = [], [], []
    for h in range(n):
        one_m_lb = 1.0 - lbs[h]
        logf = jnp.log(lbs[h] + one_m_lb * _sigmoid(xs[h]))
        kk.append(one_m_lb * _sigmoid(-xs[h]))
        qf.append(qs[h] * _sigmoid(qs[h]))
        g.append(_dot01(masks["tri"], logf))
    vb = [v.astype(BF16) for v in vs]
    o = [_dot_nt((qf[h] * jnp.exp(g[h])).astype(BF16), sts[h].astype(BF16)) for h in range(n)]
    a = [jnp.where(masks["eye"], _dot_nt(qf[h].astype(BF16), kk[h].astype(BF16)), 0.0) for h in range(n)]
    for b, second, pair in masks["levels"]:
        for h in range(n):
            e = jnp.exp(-jnp.abs(g[h] - _hgrn_ref_rows(g[h], b, masks["trow"])))
            qb = jnp.where(second, qf[h] * e, 0.0).astype(BF16)
            kb = jnp.where(second, 0.0, kk[h] * e).astype(BF16)
            a[h] = jnp.where(pair, _dot_nt(qb, kb), a[h])
    o = [o[h] + _dot(a[h].astype(BF16), vb[h]) for h in range(n)]
    st_new = []
    for h in range(n):
        g_end = g[h][c - 1:c, :]
        ke = (kk[h] * jnp.exp(g_end - g[h])).astype(BF16)
        st_new.append(sts[h] * jnp.exp(g_end) + _dot_tn(vb[h], ke))
    return o, st_new


def _hgrn_kernel(q_ref, f_ref, i_ref, lb_ref, s0_ref, o_ref, sfin_ref, st_scr, *, chunk, heads):
    step = pl.program_id(1)
    n_chunks = q_ref.shape[0] // chunk

    @pl.when(step == 0)
    def _():
        for h in range(heads):
            st_scr[h] = s0_ref[h].T

    masks = _hgrn_masks(chunk)

    def body(ci, carry):
        rows = pl.ds(pl.multiple_of(ci * chunk, chunk), chunk)
        cols = [slice(h * LANES, (h + 1) * LANES) for h in range(heads)]
        o, st_new = _hgrn_chunk([q_ref[rows, cl].astype(F32) for cl in cols], [f_ref[rows, cl] for cl in cols],
                                [i_ref[rows, cl].astype(F32) for cl in cols], [lb_ref[h:h + 1, :] for h in range(heads)],
                                [st_scr[h] for h in range(heads)], masks)
        for h in range(heads):
            o_ref[rows, cols[h]] = o[h]
            st_scr[h] = st_new[h]
        return carry

    lax.fori_loop(0, n_chunks, body, 0)

    @pl.when(step == pl.num_programs(1) - 1)
    def _():
        for h in range(heads):
            sfin_ref[h] = st_scr[h].T


def _hgrn(p16, fa, lb, s0, *, chunk, tb_pref):
    bsz, l, _ = p16.shape
    heads, dk = lb.shape
    w = heads * dk
    tb = _tile(l, tb_pref)
    kern = functools.partial(_hgrn_kernel, chunk=chunk, heads=heads)
    return pl.pallas_call(
        kern,
        grid=(bsz, l // tb),
        in_specs=[
            pl.BlockSpec((None, tb, w), lambda b, t: (b, t, P16_QA)),
            pl.BlockSpec((None, tb, w), lambda b, t: (b, t, 0)),
            pl.BlockSpec((None, tb, w), lambda b, t: (b, t, P16_IA)),
            pl.BlockSpec((heads, dk), lambda b, t: (0, 0)),
            pl.BlockSpec((None, heads, dk, dk), lambda b, t: (b, 0, 0, 0)),
        ],
        out_specs=[
            pl.BlockSpec((None, tb, w), lambda b, t: (b, t, 0)),
            pl.BlockSpec((None, heads, dk, dk), lambda b, t: (b, 0, 0, 0)),
        ],
        out_shape=[
            jax.ShapeDtypeStruct((bsz, l, w), F32),
            jax.ShapeDtypeStruct(s0.shape, F32),
        ],
        scratch_shapes=[pltpu.VMEM((heads, dk, dk), F32)],
        compiler_params=_params("parallel", "arbitrary"),
        name="hgrn2",
    )(p16, fa, p16, lb, s0)


def _fox_prompt_kernel(qi_tab, ki_tab, q_ref, k_ref, v_ref, ccol_ref, crow_ref, o_ref, m_scr, l_scr, acc_scr,
                       *, heads, dh):
    step = pl.program_id(1)
    qi = qi_tab[step]
    ki = ki_tab[step]
    tq, tk = q_ref.shape[0], k_ref.shape[0]

    @pl.when(ki == 0)
    def _():
        m_scr[...] = jnp.full_like(m_scr, -jnp.inf)
        l_scr[...] = jnp.zeros_like(l_scr)
        acc_scr[...] = jnp.zeros_like(acc_scr)

    def block(diagonal):
        sub = min(tq, FOX_SUB_ROWS)
        units = [(h, r0) for h in range(heads) for r0 in range(0, tq, sub)]

        def scores(h, r0):
            kw = r0 + sub if diagonal else tk
            return _dot_nt(q_ref[r0:r0 + sub, h * dh:(h + 1) * dh], k_ref[:kw, h * dh:(h + 1) * dh])

        s_next = scores(*units[0])
        for u, (h, r0) in enumerate(units):
            s = s_next
            if u + 1 < len(units):
                s_next = scores(*units[u + 1])
            rows = slice(r0, r0 + sub)
            cols = slice(h * dh, (h + 1) * dh)
            kw = s.shape[1]
            s = s - crow_ref[h:h + 1, :kw]
            if diagonal:
                row = lax.broadcasted_iota(jnp.int32, (sub, kw), 0) + r0
                col = lax.broadcasted_iota(jnp.int32, (sub, kw), 1)
                s = jnp.where(col <= row, s, -jnp.inf)
            cc = ccol_ref[h, rows, :]
            m_old = m_scr[h, rows, :]
            m_new = jnp.maximum(m_old, jnp.max(s, axis=-1, keepdims=True) + cc)
            p = jnp.exp2(s - jnp.concatenate([m_new - cc] * (kw // LANES), axis=1))
            alpha = jnp.exp2(m_old - m_new)
            l_new = alpha * l_scr[h, rows, :] + jnp.sum(p, axis=-1, keepdims=True)
            acc = alpha * acc_scr[rows, cols] + _dot(p.astype(BF16), v_ref[:kw, cols])
            m_scr[h, rows, :] = m_new
            if diagonal:
                o_ref[rows, cols] = acc / l_new
            else:
                l_scr[h, rows, :] = l_new
                acc_scr[rows, cols] = acc

    @pl.when(ki < qi)
    def _():
        block(False)

    @pl.when(ki == qi)
    def _():
        block(True)


def _fox_prompt(qkv16, ccol, crow, *, heads, dh):
    assert dh == LANES
    bsz, l, _ = qkv16.shape
    w = heads * dh
    t = _tile(l, FOX_TILE)
    nq = l // t
    pairs = [(qi, ki) for qi in range(nq) for ki in range(qi + 1)]
    qi_tab = jnp.asarray([p[0] for p in pairs], jnp.int32)
    ki_tab = jnp.asarray([p[1] for p in pairs], jnp.int32)
    return pl.pallas_call(
        functools.partial(_fox_prompt_kernel, heads=heads, dh=dh),
        grid_spec=pltpu.PrefetchScalarGridSpec(
            num_scalar_prefetch=2,
            grid=(bsz, len(pairs)),
            in_specs=[
                pl.BlockSpec((None, t, w), lambda b, s, qt, kt: (b, qt[s], 2)),
                pl.BlockSpec((None, t, w), lambda b, s, qt, kt: (b, kt[s], 0)),
                pl.BlockSpec((None, t, w), lambda b, s, qt, kt: (b, kt[s], 1)),
                pl.BlockSpec((None, heads, t, LANES), lambda b, s, qt, kt: (b, 0, qt[s], 0)),
                pl.BlockSpec((None, heads, t), lambda b, s, qt, kt: (b, 0, kt[s])),
            ],
            out_specs=pl.BlockSpec((None, t, w), lambda b, s, qt, kt: (b, qt[s], 0)),
            scratch_shapes=[pltpu.VMEM((heads, t, LANES), F32), pltpu.VMEM((heads, t, LANES), F32),
                            pltpu.VMEM((t, w), F32)],
        ),
        out_shape=jax.ShapeDtypeStruct((bsz, l, w), F32),
        compiler_params=_params("parallel", "arbitrary"),
        name="fox_prompt",
    )(qi_tab, ki_tab, qkv16, qkv16, qkv16, ccol, crow)


def _fox_sample_kernel(pt_ref, q_ref, kn_ref, vn_ref, lfr_ref, lfc_ref, *refs, heads, pp):
    del pt_ref
    k_refs, v_refs, lf_refs = refs[:pp], refs[pp:2 * pp], refs[2 * pp:3 * pp]
    o_ref = refs[3 * pp]
    m_scr, l_scr, acc_scr, rb_scr, carry_scr, lf_scr = refs[3 * pp + 1:]
    g = pl.program_id(1)
    n = q_ref.shape[0]
    w = k_refs[0].shape[0]
    qb = q_ref[...]

    @pl.when(g == 0)
    def _():
        ii = lax.broadcasted_iota(jnp.int32, (n, n), 0)
        jj = lax.broadcasted_iota(jnp.int32, (n, n), 1)
        same_head = (ii & (heads - 1)) == (jj & (heads - 1))
        m_row = jnp.where(same_head & (ii <= jj), 1.0, 0.0).astype(BF16)
        m_col = jnp.where(same_head & (jj <= ii), 1.0, 0.0).astype(BF16)
        cn_row = _dot01_right(jnp.broadcast_to(lfr_ref[...], (8, n)), m_row)[0:1, :] * LOG2E
        cn_col = _dot01(m_col, lfc_ref[...]) * LOG2E
        iw = lax.broadcasted_iota(jnp.int32, (n, w), 0)
        jw = lax.broadcasted_iota(jnp.int32, (n, w), 1)
        rb_scr[...] = jnp.where((iw & (heads - 1)) == (jw & (heads - 1)), cn_col, -jnp.inf)
        carry_scr[...] = jnp.zeros_like(carry_scr)
        s = _dot_nt(qb, kn_ref[...]) + cn_col - cn_row
        s = jnp.where(same_head & (jj <= ii), s, -jnp.inf)
        m0 = jnp.max(s, axis=-1, keepdims=True)
        p = jnp.exp2(s - m0)
        m_scr[...] = m0
        l_scr[...] = jnp.sum(p, axis=-1, keepdims=True)
        acc_scr[...] = _dot(p.astype(BF16), vn_ref[...])

    for j in range(pp):
        lf_scr[j:j + 1, :] = lf_refs[j][...]
    x = lf_scr[...] * LOG2E
    lane = lax.broadcasted_iota(jnp.int32, (pp, w), 1)
    incl, tot = x, x
    d = heads
    while d < w:
        incl = incl + jnp.where(lane < w - d, pltpu.roll(incl, w - d, 1), 0.0)
        tot = tot + pltpu.roll(tot, d, 1)
        d *= 2
    excl = jnp.where(lane < w - heads, pltpu.roll(incl, w - heads, 1), 0.0)
    run = carry_scr[...]
    bias = [None] * pp
    for j in reversed(range(pp)):
        bias[j] = excl[j:j + 1, :] + run
        run = run + tot[j:j + 1, :]
    carry_scr[...] = run

    rb = rb_scr[...]
    s_list = [_dot_nt(qb, k_refs[j][...].astype(BF16)) + rb + bias[j] for j in range(pp)]
    mx = s_list[0]
    for s in s_list[1:]:
        mx = jnp.maximum(mx, s)
    m_old = m_scr[...]
    m_new = jnp.maximum(m_old, jnp.max(mx, axis=-1, keepdims=True))
    alpha = jnp.exp2(m_old - m_new)
    l_new = alpha * l_scr[...]
    acc = alpha * acc_scr[...]
    for j in range(pp):
        p = jnp.exp2(s_list[j] - m_new)
        l_new = l_new + jnp.sum(p, axis=-1, keepdims=True)
        acc = acc + _dot(p.astype(BF16), v_refs[j][...].astype(BF16))
    m_scr[...] = m_new
    l_scr[...] = l_new
    acc_scr[...] = acc

    @pl.when(g == pl.num_programs(1) - 1)
    def _():
        o_ref[...] = acc / l_new


def _fox_sample(q16, k16, v16, lf_row, lf_col, cache_k, cache_v, cache_lf, page_table, layer, *, heads, pp):
    bd, n, dh = q16.shape
    w = cache_k.shape[2]
    n_pages = page_table.shape[1]
    assert n_pages % pp == 0
    steps = n_pages // pp
    tok = lambda b, g, pt: (b, 0, 0)

    def page(j):
        return lambda b, g, pt: (layer, pt[b, n_pages - pp * (g + 1) + j], 0, 0)

    return pl.pallas_call(
        functools.partial(_fox_sample_kernel, heads=heads, pp=pp),
        grid_spec=pltpu.PrefetchScalarGridSpec(
            num_scalar_prefetch=1,
            grid=(bd, steps),
            in_specs=[
                pl.BlockSpec((None, n, dh), tok),
                pl.BlockSpec((None, n, dh), tok),
                pl.BlockSpec((None, n, dh), tok),
                pl.BlockSpec((None, 1, n), tok),
                pl.BlockSpec((None, n, 1), tok),
            ] + [pl.BlockSpec((None, None, w, dh), page(j)) for j in range(pp)]
              + [pl.BlockSpec((None, None, w, dh), page(j)) for j in range(pp)]
              + [pl.BlockSpec((None, None, 1, w), page(j)) for j in range(pp)],
            out_specs=pl.BlockSpec((None, n, dh), tok),
            scratch_shapes=[pltpu.VMEM((n, 1), F32), pltpu.VMEM((n, 1), F32), pltpu.VMEM((n, dh), F32),
                            pltpu.VMEM((n, w), F32), pltpu.VMEM((1, w), F32), pltpu.VMEM((pp, w), F32)],
        ),
        out_shape=jax.ShapeDtypeStruct((bd, n, dh), F32),
        compiler_params=_params("parallel", "arbitrary"),
        name="fox_sample",
    )(page_table, q16, k16, v16, lf_row, lf_col, *([cache_k] * pp), *([cache_v] * pp), *([cache_lf] * pp))


def _merge_kernel(x_ref, oa_ref, ob_ref, za_ref, zb_ref, ga_ref, gb_ref, ng_ref, wa_ref, wb_ref, wo_ref,
                  pg_ref, y_ref, *, heads):
    ng = ng_ref[...]
    parts = []
    for h in range(heads):
        cols = slice(h * LANES, (h + 1) * LANES)
        blk = oa_ref[:, cols]
        nrm = blk * lax.rsqrt(jnp.mean(blk * blk, axis=-1, keepdims=True) + RMS_EPS) * ng
        za = za_ref[:, cols].astype(F32)
        parts.append((nrm * (za * _sigmoid(za))).astype(BF16))
    ya = jnp.concatenate(parts, axis=-1)
    zb = zb_ref[...].astype(F32)
    yb = (ob_ref[...] * (zb * _sigmoid(zb))).astype(BF16)
    u = (_sigmoid(ga_ref[...].astype(F32)) * _dot(ya, wa_ref[...])
         + _sigmoid(gb_ref[...].astype(F32)) * _dot(yb, wb_ref[...]))
    mix = _dot(u.astype(BF16), wo_ref[...])
    nrm = mix * lax.rsqrt(jnp.mean(mix * mix, axis=-1, keepdims=True) + RMS_EPS) * pg_ref[...]
    y_ref[...] = x_ref[...] + nrm


def _merge(x, oa, ob, p16, norm_g, w_a, w_b, w_out, post_g, layer, *, heads, tm_pref):
    m, d = x.shape
    w = oa.shape[1]
    tm = _tile(m, tm_pref)
    assert (P16_G * w) % d == 0
    g_col = (P16_G * w) // d
    row = lambda i: (i, 0)
    lay = lambda i: (layer, 0, 0)
    return pl.pallas_call(
        functools.partial(_merge_kernel, heads=heads),
        grid=(m // tm,),
        in_specs=[
            pl.BlockSpec((tm, d), row),
            pl.BlockSpec((tm, w), row),
            pl.BlockSpec((tm, w), row),
            pl.BlockSpec((tm, w), lambda i: (i, P16_ZA)),
            pl.BlockSpec((tm, w), lambda i: (i, P16_ZB)),
            pl.BlockSpec((tm, d), lambda i: (i, g_col)),
            pl.BlockSpec((tm, d), lambda i: (i, g_col + 1)),
            pl.BlockSpec((None, 1, LANES), lay),
            pl.BlockSpec((None,) + w_a.shape[1:], lay),
            pl.BlockSpec((None,) + w_b.shape[1:], lay),
            pl.BlockSpec((None,) + w_out.shape[1:], lay),
            pl.BlockSpec((None, 1, d), lay),
        ],
        out_specs=pl.BlockSpec((tm, d), row),
        out_shape=jax.ShapeDtypeStruct((m, d), F32),
        compiler_params=_params("parallel"),
        name="merge_out",
    )(x, oa, ob, p16, p16, p16, p16, norm_g, w_a, w_b, w_out, post_g)


def kernel(x_prompt, x_sample, state_hgrn, cache_k, cache_v, cache_logf, page_table, w_in, hgrn_lower_bounds,
           hgrn_norm_g, fox_f_bias, w_branch_a, w_branch_b, w_out, pre_norm_g, post_norm_g):
    bp, seq, d = x_prompt.shape
    bd, tdec, _ = x_sample.shape
    depth, _, heads, dk, dv = state_hgrn.shape
    _, n_pool, page, hb, dh = cache_k.shape
    assert heads == hb and dk == dv == dh == LANES
    w = heads * dk
    n_pages = page_table.shape[1]
    q_scale = LOG2E * dh ** -0.5

    lb_all = _lower_bounds(hgrn_lower_bounds).reshape(depth, heads, dk)

    w_in_t = jnp.swapaxes(w_in, 1, 2)
    w_main = w_in_t[:, :N_MAIN * w, :].astype(BF16)
    w_g = w_in_t[:, N_MAIN * w + heads:, :].astype(BF16)
    w_f = jnp.pad(w_in_t[:, N_MAIN * w:N_MAIN * w + heads, :], ((0, 0), (0, LANES - heads), (0, 0))).astype(BF16)
    f_bias = jnp.pad(fox_f_bias.astype(F32), ((0, 0), (0, LANES - heads)))[:, None, :]
    w_a16 = w_branch_a.astype(BF16)
    w_b16 = w_branch_b.astype(BF16)
    w_o16 = w_out.astype(BF16)
    pre_g = pre_norm_g.astype(F32)[:, None, :]
    post_g = post_norm_g.astype(F32)[:, None, :]
    norm_g = hgrn_norm_g.astype(F32)[:, None, :]

    ck = cache_k.reshape(depth, n_pool, page * heads, dh)
    cv = cache_v.reshape(depth, n_pool, page * heads, dh)
    clf = cache_logf.astype(F32).reshape(depth, n_pool, 1, page * heads)

    mp, ms = bp * seq, bd * tdec
    xp = x_prompt.reshape(mp, d)
    xs = x_sample.reshape(ms, d)
    zeros_state = jnp.zeros((bp, heads, dk, dv), F32)
    assert seq % HGRN_CHUNK == 0 and (tdec & (tdec - 1)) == 0
    kp = lax.empty((depth, mp, w), F32)
    vp = lax.empty((depth, mp, w), F32)
    ks = lax.empty((depth, ms, w), F32)
    vs = lax.empty((depth, ms, w), F32)
    nh = tdec * heads

    st_p, st_s, lfp_l, lfs_l = [], [], [], []
    for l in range(depth):
        fa, lf, qkv16, p16, kp, vp = _in_proj(xp, pre_g, w_main, w_g, w_f, f_bias, kp, vp, l,
                                              group_w=w, q_scale=q_scale)
        oa, s_fin = _hgrn(p16.reshape(bp, seq, -1), fa.reshape(bp, seq, w), lb_all[l], zeros_state,
                          chunk=HGRN_CHUNK, tb_pref=HGRN_ROWS)
        ccol, crow = _seq_cumsum(lf.reshape(bp, seq, LANES), heads)
        ob = _fox_prompt(qkv16.reshape(bp, seq, 3 * w), ccol, crow, heads=heads, dh=dh)
        xp = _merge(xp, oa.reshape(mp, w), ob.reshape(mp, w), p16, norm_g, w_a16, w_b16, w_o16, post_g, l,
                    heads=heads, tm_pref=TM_MERGE)
        st_p.append(s_fin)
        lfp_l.append(lf[:, :heads].reshape(bp, seq, heads))

        fa, lf, qkv16, p16, ks, vs = _in_proj(xs, pre_g, w_main, w_g, w_f, f_bias, ks, vs, l,
                                              group_w=w, q_scale=q_scale)
        oa, s_fin = _hgrn(p16.reshape(bd, tdec, -1), fa.reshape(bd, tdec, w), lb_all[l],
                          state_hgrn[l].astype(F32), chunk=tdec, tb_pref=tdec)
        lf8 = lf[:, :heads].reshape(bd, nh)
        ob = _fox_sample(qkv16[:, 2 * w:].reshape(bd, nh, dh), qkv16[:, :w].reshape(bd, nh, dh),
                         qkv16[:, w:2 * w].reshape(bd, nh, dh), lf8[:, None, :], lf8[:, :, None],
                         ck, cv, clf, page_table, l, heads=heads, pp=_tile(n_pages, PAGES_PER_STEP))
        xs = _merge(xs, oa.reshape(ms, w), ob.reshape(ms, w), p16, norm_g, w_a16, w_b16, w_o16, post_g, l,
                    heads=heads, tm_pref=TM_MERGE)
        st_s.append(s_fin)
        lfs_l.append(lf[:, :heads].reshape(bd, tdec, heads))

    return (xp.reshape(bp, seq, d), xs.reshape(bd, tdec, d), jnp.stack(st_p), jnp.stack(st_s),
            kp.reshape(depth, bp, seq, heads, dh), vp.reshape(depth, bp, seq, heads, dh), jnp.stack(lfp_l),
            ks.reshape(depth, bd, tdec, heads, dh), vs.reshape(depth, bd, tdec, heads, dh), jnp.stack(lfs_l))
```
